```python
import jax, jax.numpy as jnp
from jax import lax
import numpy as np

D_MODEL = 1024
BATCH = 8
SEQ = 2048
DEPTH = 2
DEC_BATCH = 128
DEC_SEQ = 1
PAST_LEN = 16384
PAGE_SIZE = 128

GM_HEADS = 4
GM_HEAD_DIM = D_MODEL // 8
GM_WIDTH = GM_HEADS * GM_HEAD_DIM
GM_CHUNK = 128
ML_HEADS = 4
ML_HEAD_DIM = D_MODEL // 8
ML_WIDTH = ML_HEADS * ML_HEAD_DIM
ML_CONV = 4
ML_CHUNK = 64
D_MIX = GM_WIDTH + ML_WIDTH
N_IN = 2 * GM_WIDTH + 2 * ML_WIDTH + 2 * ML_HEADS
MEM_TOKENS = 256
XA_HEADS = 4
XA_HEAD_DIM = D_MODEL // XA_HEADS
D_FF = 4 * D_MODEL
EPS = 1e-6
NEG = -1e30

kernel_name = "hybrid_gmlp_mlstm_memxattn_step"


def _rmsnorm(x, g):
    xf = x.astype(jnp.float32)
    y = xf * lax.rsqrt(jnp.mean(xf * xf, axis=-1, keepdims=True) + EPS)
    return (y * g.astype(jnp.float32)).astype(x.dtype)


def _chunk_gmlp(z, v_norm_g, ws, bs):
    z = jax.nn.gelu(z)
    u, v = jnp.split(z, 2, axis=-1)
    v = _rmsnorm(v, v_norm_g)
    B, S, _ = v.shape
    n_chunks = -(-S // GM_CHUNK)
    pad = n_chunks * GM_CHUNK - S
    vp = jnp.pad(v, ((0, 0), (0, pad), (0, 0))).reshape(B, n_chunks, GM_CHUNK, GM_HEADS, GM_HEAD_DIM)
    causal = jnp.tril(jnp.ones((GM_CHUNK, GM_CHUNK), dtype=bool))
    w = jnp.where(causal[None], ws, 0).astype(vp.dtype)
    mixed = jnp.einsum('hts,bnshd->bnthd', w, vp) + bs.T[None, None, :, :, None]
    gate = mixed.reshape(B, n_chunks * GM_CHUNK, GM_WIDTH)[:, :S]
    return u * gate, v


def _causal_conv(x_ext, w, b, S):
    out = b
    for j in range(ML_CONV):
        out = out + x_ext[:, j:j + S] * w[j]
    return out


def _mlstm(q, k, v, ig, lf, C0, n0, m0):
    B, S, H, D = q.shape
    L = min(ML_CHUNK, S)
    nc = -(-S // L)
    pad = nc * L - S
    valid = (jnp.arange(nc * L) < S)[None, :, None]
    padf = lambda a: jnp.pad(a, [(0, 0), (0, pad)] + [(0, 0)] * (a.ndim - 2))
    q, k, v = padf(q), padf(k), padf(v)
    ig = jnp.where(valid, padf(ig), NEG)
    lf = jnp.where(valid, padf(lf), 0.0)
    to_chunks = lambda a: jnp.moveaxis(a.reshape((B, nc, L) + a.shape[2:]), 1, 0)
    causal = jnp.tril(jnp.ones((L, L), dtype=bool))[None, :, :, None]

    def step(carry, inp):
        C, n, m = carry
        qc, kc, vc, igc, lfc = inp
        b = jnp.cumsum(lfc, axis=1)
        a = b + m[:, None, :]
        dlog = jnp.where(causal, b[:, :, None, :] - b[:, None, :, :] + igc[:, None, :, :], NEG)
        mt = jnp.maximum(a, jnp.max(dlog, axis=2))
        w_inter = jnp.exp(a - mt)
        s = jnp.einsum('bthd,bshd->btsh', qc, kc) * jnp.exp(dlog - mt[:, :, None, :])
        num = jnp.einsum('btsh,bshd->bthd', s, vc) + w_inter[..., None] * jnp.einsum('bthk,bhkv->bthv', qc, C)
        den = jnp.sum(s, axis=2) + w_inter * jnp.einsum('bthk,bhk->bth', qc, n)
        h = num / jnp.maximum(jnp.abs(den), jnp.exp(-mt))[..., None]
        bL = b[:, -1]
        aL = bL + m
        wlog = bL[:, None, :] - b + igc
        m_new = jnp.maximum(aL, jnp.max(wlog, axis=1))
        w_s = jnp.exp(wlog - m_new[:, None, :])
        decay = jnp.exp(aL - m_new)
        C_new = decay[..., None, None] * C + jnp.einsum('bsh,bshk,bshv->bhkv', w_s, kc, vc)
        n_new = decay[..., None] * n + jnp.einsum('bsh,bshk->bhk', w_s, kc)
        return (C_new, n_new, m_new), h

    (C, n, m), h = lax.scan(step, (C0, n0, m0), tuple(map(to_chunks, (q, k, v, ig, lf))))
    h = jnp.moveaxis(h, 0, 1).reshape(B, nc * L, H, D)[:, :S]
    return h, C, n, m


def _mem_kv(mem, g_mem, w_ck, w_cv):
    B = mem.shape[0]
    mn = _rmsnorm(mem, g_mem)
    mk = (mn @ w_ck).reshape(B, MEM_TOKENS, XA_HEADS, XA_HEAD_DIM)
    mv = (mn @ w_cv).reshape(B, MEM_TOKENS, XA_HEADS, XA_HEAD_DIM)
    return mk, mv


def _layer(x, mem_k, mem_v, conv_buf, C0, n0, m0, lw):
    (g_mix, w_in, gm_v_g, gm_ws, gm_bs, ml_conv_w, ml_conv_b, ml_wq, ml_wk, ml_wv,
     ml_b_i, ml_b_f, ml_out_g, ml_skip, w_out, g_xa, w_cq, w_co, g_ffn, w_up, w_down) = lw
    f32 = jnp.float32
    B, S, _ = x.shape
    h = _rmsnorm(x, g_mix)
    proj = h @ w_in
    c0 = 2 * GM_WIDTH
    c1 = c0 + ML_WIDTH
    c2 = c1 + ML_WIDTH
    c3 = c2 + ML_HEADS
    z_gm, xm, o_pre, ig_pre, fg_pre = jnp.split(proj, [c0, c1, c2, c3], axis=-1)
    y_gm, v_gm = _chunk_gmlp(z_gm, gm_v_g, gm_ws, gm_bs)
    x_ext = jnp.concatenate([conv_buf.astype(xm.dtype), xm], axis=1)
    conv_act = jax.nn.silu(_causal_conv(x_ext, ml_conv_w, ml_conv_b, S))
    heads = lambda a: a.reshape(B, S, ML_HEADS, ML_HEAD_DIM)
    q = jnp.einsum('bshd,hde->bshe', heads(conv_act), ml_wq)
    k = jnp.einsum('bshd,hde->bshe', heads(conv_act), ml_wk) * (ML_HEAD_DIM ** -0.5)
    v = jnp.einsum('bshd,hde->bshe', heads(xm), ml_wv)
    ig = (ig_pre + ml_b_i).astype(f32)
    lf = jax.nn.log_sigmoid((fg_pre + ml_b_f).astype(f32))
    hc, C, n, m = _mlstm(q.astype(f32), k.astype(f32), v.astype(f32), ig, lf,
                         C0.astype(f32), n0.astype(f32), m0.astype(f32))
    hc = _rmsnorm(hc, ml_out_g.reshape(ML_HEADS, ML_HEAD_DIM)).reshape(B, S, ML_WIDTH).astype(x.dtype)
    y_ml = jax.nn.sigmoid(o_pre) * (hc + ml_skip * conv_act)
    x = x + jnp.concatenate([y_gm, y_ml], axis=-1) @ w_out
    hq = (_rmsnorm(x, g_xa) @ w_cq).reshape(B, S, XA_HEADS, XA_HEAD_DIM)
    sc = jnp.einsum('bshd,bmhd->bhsm', hq, mem_k.astype(hq.dtype)).astype(f32) * (XA_HEAD_DIM ** -0.5)
    p = jax.nn.softmax(sc, axis=-1).astype(x.dtype)
    att = jnp.einsum('bhsm,bmhd->bshd', p, mem_v.astype(x.dtype)).reshape(B, S, D_MODEL)
    x = x + att @ w_co
    hf = _rmsnorm(x, g_ffn)
    x = x + jnp.square(jax.nn.relu(hf @ w_up)) @ w_down
    new_buf = x_ext[:, -(ML_CONV - 1):]
    return x, v_gm, new_buf, C, n, m


def setup_inputs(seed: int = 0) -> dict:
    key = jax.random.key(seed)
    ks = iter(jax.random.split(key, 48))
    f32 = jnp.float32
    nrm = lambda shape, s: s * jax.random.normal(next(ks), shape, f32)
    gain = lambda shape: 1.0 + nrm(shape, 0.02)
    H, Dh = ML_HEADS, ML_HEAD_DIM
    return {
        "x_prompt": nrm((BATCH, SEQ, D_MODEL), 1.0),
        "x_sample": nrm((DEC_BATCH, DEC_SEQ, D_MODEL), 1.0),
        "mem_prompt": nrm((BATCH, MEM_TOKENS, D_MODEL), 1.0),
        "cache_mem_k": nrm((DEPTH, DEC_BATCH, MEM_TOKENS, XA_HEADS, XA_HEAD_DIM), 1.0),
        "cache_mem_v": nrm((DEPTH, DEC_BATCH, MEM_TOKENS, XA_HEADS, XA_HEAD_DIM), 1.0),
        "state_C": nrm((DEPTH, DEC_BATCH, H, Dh, Dh), 0.1),
        "state_n": nrm((DEPTH, DEC_BATCH, H, Dh), 0.1),
        "state_m": 2.0 + nrm((DEPTH, DEC_BATCH, H), 0.5),
        "state_conv": nrm((DEPTH, DEC_BATCH, ML_CONV - 1, ML_WIDTH), 1.0),
        "norm_mix_g": gain((DEPTH, D_MODEL)),
        "w_in": nrm((DEPTH, D_MODEL, N_IN), D_MODEL ** -0.5),
        "gm_v_norm_g": gain((DEPTH, GM_WIDTH)),
        "gm_ws": nrm((DEPTH, GM_HEADS, GM_CHUNK, GM_CHUNK), GM_CHUNK ** -0.5),
        "gm_bs": 1.0 + nrm((DEPTH, GM_HEADS, GM_CHUNK), 0.1),
        "ml_conv_w": nrm((DEPTH, ML_CONV, ML_WIDTH), 0.5),
        "ml_conv_b": nrm((DEPTH, ML_WIDTH), 0.01),
        "ml_wq": nrm((DEPTH, H, Dh, Dh), Dh ** -0.5),
        "ml_wk": nrm((DEPTH, H, Dh, Dh), Dh ** -0.5),
        "ml_wv": nrm((DEPTH, H, Dh, Dh), Dh ** -0.5),
        "ml_b_i": nrm((DEPTH, H), 0.1),
        "ml_b_f": jnp.linspace(3.0, 6.0, H, dtype=f32)[None, :] + nrm((DEPTH, H), 0.1),
        "ml_out_norm_g": gain((DEPTH, ML_WIDTH)),
        "ml_skip": 1.0 + nrm((DEPTH, ML_WIDTH), 0.1),
        "w_out": nrm((DEPTH, D_MIX, D_MODEL), D_MIX ** -0.5),
        "norm_mem_g": gain((DEPTH, D_MODEL)),
        "w_ck": nrm((DEPTH, D_MODEL, D_MODEL), D_MODEL ** -0.5),
        "w_cv": nrm((DEPTH, D_MODEL, D_MODEL), D_MODEL ** -0.5),
        "norm_xa_g": gain((DEPTH, D_MODEL)),
        "w_cq": nrm((DEPTH, D_MODEL, D_MODEL), D_MODEL ** -0.5),
        "w_co": nrm((DEPTH, D_MODEL, D_MODEL), D_MODEL ** -0.5),
        "norm_ffn_g": gain((DEPTH, D_MODEL)),
        "w_up": nrm((DEPTH, D_MODEL, D_FF), D_MODEL ** -0.5),
        "w_down": nrm((DEPTH, D_FF, D_MODEL), D_FF ** -0.5),
        "norm_f_g": gain((D_MODEL,)),
    }


def reference(x_prompt, x_sample, mem_prompt, cache_mem_k, cache_mem_v, state_C, state_n, state_m, state_conv,
              norm_mix_g, w_in, gm_v_norm_g, gm_ws, gm_bs, ml_conv_w, ml_conv_b, ml_wq, ml_wk, ml_wv,
              ml_b_i, ml_b_f, ml_out_norm_g, ml_skip, w_out, norm_mem_g, w_ck, w_cv, norm_xa_g, w_cq, w_co,
              norm_ffn_g, w_up, w_down, norm_f_g):
    f32 = jnp.float32
    B = x_prompt.shape[0]
    xp, xs = x_prompt, x_sample
    mk_p, mv_p, C_p, n_p, m_p, cv_p = [], [], [], [], [], []
    C_s, n_s, m_s, cv_s, gv_s = [], [], [], [], []
    for l in range(DEPTH):
        lw = (norm_mix_g[l], w_in[l], gm_v_norm_g[l], gm_ws[l], gm_bs[l], ml_conv_w[l], ml_conv_b[l],
              ml_wq[l], ml_wk[l], ml_wv[l], ml_b_i[l], ml_b_f[l], ml_out_norm_g[l], ml_skip[l], w_out[l],
              norm_xa_g[l], w_cq[l], w_co[l], norm_ffn_g[l], w_up[l], w_down[l])
        mk, mv = _mem_kv(mem_prompt, norm_mem_g[l], w_ck[l], w_cv[l])
        xp, _, bp, Cp, np_, mp = _layer(
            xp, mk, mv,
            jnp.zeros((B, ML_CONV - 1, ML_WIDTH), xp.dtype),
            jnp.zeros((B, ML_HEADS, ML_HEAD_DIM, ML_HEAD_DIM), f32),
            jnp.zeros((B, ML_HEADS, ML_HEAD_DIM), f32),
            jnp.zeros((B, ML_HEADS), f32), lw)
        mk_p.append(mk); mv_p.append(mv); C_p.append(Cp); n_p.append(np_); m_p.append(mp); cv_p.append(bp)
        xs, vs, bsm, Cs, ns, ms = _layer(
            xs, cache_mem_k[l], cache_mem_v[l], state_conv[l], state_C[l], state_n[l], state_m[l], lw)
        C_s.append(Cs); n_s.append(ns); m_s.append(ms); cv_s.append(bsm); gv_s.append(vs)
    y_prompt = _rmsnorm(xp, norm_f_g)
    y_sample = _rmsnorm(xs, norm_f_g)
    st = lambda a: jnp.stack(a, axis=0)
    return (y_prompt, y_sample, st(mk_p), st(mv_p), st(C_p), st(n_p), st(m_p), st(cv_p),
            st(C_s), st(n_s), st(m_s), st(cv_s), st(gv_s))
```

```python
import functools

import jax
import jax.numpy as jnp
from jax import lax
from jax.experimental import pallas as pl
from jax.experimental.pallas import tpu as pltpu

F32 = jnp.float32
MXU_DTYPE = jnp.bfloat16

EPS = 1e-6
NEG = -1e30
LANES = 128
SUBLANES = 8
HEADS = 4
HEAD_DIM = 128
GM_CHUNK = 128
ML_CHUNK = 128
ML_CONV = 4
MIX_TILE = 256
XA_TILE = 256
FFN_TILE = 512
FFN_CHUNK = 1024
STATE_BLOCK = 8
VMEM_LIMIT = 56 * 1024 * 1024


def _cparams(*sem):
    return pltpu.CompilerParams(dimension_semantics=sem, vmem_limit_bytes=VMEM_LIMIT)


def _mx(a):
    return a.astype(MXU_DTYPE)


def _dot(a, b):
    return jnp.dot(_mx(a), _mx(b), preferred_element_type=F32)


def _dot_nt(a, b):
    return lax.dot_general(_mx(a), _mx(b), (((1,), (1,)), ((), ())), preferred_element_type=F32)


def _rms(x, g):
    return x * lax.rsqrt(jnp.mean(x * x, axis=-1, keepdims=True) + EPS) * g


def _log_sigmoid(x):
    return jnp.minimum(x, 0.0) - jnp.log1p(jnp.exp(-jnp.abs(x)))


def _split3(x):
    hi = x.astype(MXU_DTYPE)
    r1 = x - hi.astype(F32)
    mid = r1.astype(MXU_DTYPE)
    lo = (r1 - mid.astype(F32)).astype(MXU_DTYPE)
    return hi, mid, lo


def _full_spec(shape):
    n = len(shape)
    return pl.BlockSpec(shape, lambda *_: (0,) * n)


def _memkv_kernel(m_ref, g_ref, wk_ref, wv_ref, k_ref, v_ref):
    hn = _mx(_rms(m_ref[...], g_ref[...]))
    k_ref[...] = jnp.dot(hn, wk_ref[...], preferred_element_type=F32)
    v_ref[...] = jnp.dot(hn, wv_ref[...], preferred_element_type=F32)


def _memkv(mem2d, g, wk, wv):
    M, D = mem2d.shape
    tm = min(M, 512)
    return pl.pallas_call(
        _memkv_kernel,
        grid=(M // tm,),
        in_specs=[pl.BlockSpec((tm, D), lambda i: (i, 0)), _full_spec((1, D)),
                  _full_spec(wk.shape), _full_spec(wv.shape)],
        out_specs=[pl.BlockSpec((tm, D), lambda i: (i, 0))] * 2,
        out_shape=[jax.ShapeDtypeStruct((M, D), F32)] * 2,
        compiler_params=_cparams("parallel"),
        name="memkv",
    )(mem2d, g, wk, wv)


def _mix_kernel(x_ref, gmix_ref, win_ref, wg_ref, wgt_ref, gbrow_ref, gbcol_ref, gmvg_ref, ws_ref, bst_ref,
                cw_ref, cb_ref, wq_ref, wk_ref, wv_ref, outg_ref, skip_ref, wout_ref,
                xo_ref, c_ref, n_ref, m_ref, conv_ref, xe_s, y_s, *, tile, width):
    W = width
    L = ML_CHUNK
    D = HEAD_DIM

    @pl.when(pl.program_id(1) == 0)
    def _init():
        xe_s[0:SUBLANES, :] = jnp.zeros((SUBLANES, W), F32)
        c_ref[...] = jnp.zeros_like(c_ref)
        n_ref[...] = jnp.zeros_like(n_ref)
        m_ref[...] = jnp.zeros_like(m_ref)

    x = x_ref[0]
    hn = _mx(_rms(x, gmix_ref[...]))
    proj = jnp.dot(hn, win_ref[...], preferred_element_type=F32)
    g_col = jnp.dot(hn, wg_ref[...], preferred_element_type=F32) + gbrow_ref[...]
    g_row = _dot_nt(wgt_ref[...], hn) + gbcol_ref[...]
    ig_col = g_col[:, :LANES]
    lf_col = _log_sigmoid(g_col[:, LANES:])
    ig_row = g_row[:SUBLANES]
    lf_row = _log_sigmoid(g_row[SUBLANES:])

    z = jax.nn.gelu(proj[:, :2 * W])
    u = z[:, :W]
    v_gm = _rms(z[:, W:], gmvg_ref[...])
    row = lax.broadcasted_iota(jnp.int32, (GM_CHUNK, GM_CHUNK), 0)
    col = lax.broadcasted_iota(jnp.int32, (GM_CHUNK, GM_CHUNK), 1)
    causal = col <= row
    for h in range(HEADS):
        w_h = _mx(jnp.where(causal, ws_ref[h], 0.0))
        b_h = bst_ref[:, h:h + 1]
        for c in range(tile // GM_CHUNK):
            rs = slice(c * GM_CHUNK, (c + 1) * GM_CHUNK)
            cs = slice(h * D, (h + 1) * D)
            mixed = jnp.dot(w_h, _mx(v_gm[rs, cs]), preferred_element_type=F32) + b_h
            y_s[rs, cs] = u[rs, cs] * mixed

    xm = proj[:, 2 * W:3 * W]
    o_gate = jax.nn.sigmoid(proj[:, 3 * W:])
    xe_s[SUBLANES:SUBLANES + tile, :] = xm
    conv = cb_ref[...] + cw_ref[ML_CONV - 1:ML_CONV, :] * xm
    for j in range(1, ML_CONV):
        conv = conv + cw_ref[ML_CONV - 1 - j:ML_CONV - j, :] * xe_s[SUBLANES - j:SUBLANES - j + tile, :]
    ca = conv * jax.nn.sigmoid(conv)
    conv_ref[0] = xe_s[SUBLANES + tile - (ML_CONV - 1):SUBLANES + tile, :]
    xe_s[0:SUBLANES, :] = xe_s[tile:tile + SUBLANES, :]

    tril = _mx(jnp.where(col <= row, 1.0, 0.0))
    triu = _mx(jnp.where(row <= col, 1.0, 0.0))
    k_scale = D ** -0.5
    for c in range(tile // L):
        rs = slice(c * L, (c + 1) * L)
        b_cols = sum(jnp.dot(tril, p, preferred_element_type=F32) for p in _split3(lf_col[rs]))
        b_rows = sum(jnp.dot(p, triu, preferred_element_type=F32) for p in _split3(lf_row[:, rs]))
        for h in range(HEADS):
            cs = slice(h * D, (h + 1) * D)
            ca_h = _mx(ca[rs, cs])
            q = jnp.dot(ca_h, wq_ref[h], preferred_element_type=F32)
            k = jnp.dot(ca_h, wk_ref[h], preferred_element_type=F32) * k_scale
            v = jnp.dot(_mx(xm[rs, cs]), wv_ref[h], preferred_element_type=F32)
            b_c = b_cols[:, h:h + 1]
            i_c = ig_col[rs, h:h + 1]
            b_r = b_rows[h:h + 1, :]
            i_r = ig_row[h:h + 1, rs]
            m_prev = m_ref[0, h:h + 1, 0:1]
            C = c_ref[0, h]
            n_row = n_ref[0, h:h + 1, :]

            a_c = b_c + m_prev
            dlog = jnp.where(causal, b_c - b_r + i_r, NEG)
            mt = jnp.maximum(a_c, jnp.max(dlog, axis=1, keepdims=True))
            w_inter = jnp.exp(a_c - mt)
            s = _dot_nt(q, k) * jnp.exp(dlog - mt)
            num = _dot(s, v) + w_inter * _dot(q, C)
            den = jnp.sum(s, axis=1, keepdims=True) + w_inter * jnp.sum(q * n_row, axis=1, keepdims=True)
            hc = num / jnp.maximum(jnp.abs(den), jnp.exp(-mt))

            b_last = b_c[L - 1:L, :]
            a_last = b_last + m_prev
            wlog = b_last - b_c + i_c
            m_new = jnp.maximum(a_last, jnp.max(wlog, axis=0, keepdims=True))
            kw = k * jnp.exp(wlog - m_new)
            decay = jnp.exp(a_last - m_new)
            c_ref[0, h] = decay * C + _dot(jnp.transpose(kw), v)
            n_ref[0, h:h + 1, :] = decay * n_row + jnp.sum(kw, axis=0, keepdims=True)
            m_ref[0, h:h + 1, :] = jnp.broadcast_to(m_new, (1, LANES))

            hcn = _rms(hc, outg_ref[:, cs])
            y_s[rs, W + h * D:W + (h + 1) * D] = o_gate[rs, cs] * (hcn + skip_ref[:, cs] * ca[rs, cs])

    xo_ref[0] = x + jnp.dot(_mx(y_s[...]), wout_ref[...], preferred_element_type=F32)


def _mix(x, p):
    B, S, Dm = x.shape
    W = HEADS * HEAD_DIM
    tile = min(S, MIX_TILE)
    assert S % tile == 0 and tile % GM_CHUNK == 0 and tile % ML_CHUNK == 0
    weights = (p["g_mix"], p["w_in_main"], p["w_g"], p["w_gt"], p["gb_row"], p["gb_col"], p["gm_v_g"], p["gm_ws"],
               p["gm_bst"], p["conv_w"], p["conv_b"], p["wq"], p["wk"], p["wv"], p["out_g"], p["skip"], p["w_out"])
    return pl.pallas_call(
        functools.partial(_mix_kernel, tile=tile, width=W),
        grid=(B, S // tile),
        in_specs=[pl.BlockSpec((1, tile, Dm), lambda b, s: (b, s, 0))] + [_full_spec(w.shape) for w in weights],
        out_specs=[pl.BlockSpec((1, tile, Dm), lambda b, s: (b, s, 0)),
                   pl.BlockSpec((1, HEADS, HEAD_DIM, HEAD_DIM), lambda b, s: (b, 0, 0, 0)),
                   pl.BlockSpec((1, HEADS, HEAD_DIM), lambda b, s: (b, 0, 0)),
                   pl.BlockSpec((1, HEADS, LANES), lambda b, s: (b, 0, 0)),
                   pl.BlockSpec((1, ML_CONV - 1, W), lambda b, s: (b, 0, 0))],
        out_shape=[jax.ShapeDtypeStruct((B, S, Dm), F32),
                   jax.ShapeDtypeStruct((B, HEADS, HEAD_DIM, HEAD_DIM), F32),
                   jax.ShapeDtypeStruct((B, HEADS, HEAD_DIM), F32),
                   jax.ShapeDtypeStruct((B, HEADS, LANES), F32),
                   jax.ShapeDtypeStruct((B, ML_CONV - 1, W), F32)],
        scratch_shapes=[pltpu.VMEM((tile + SUBLANES, W), F32), pltpu.VMEM((tile, 2 * W), F32)],
        compiler_params=_cparams("parallel", "arbitrary"),
        name="mix",
    )(x, *weights)


def _xattn_kernel(x_ref, g_ref, wq_ref, k_ref, v_ref, wo_ref, o_ref, *, heads):
    x = x_ref[0]
    hq = jnp.dot(_mx(_rms(x, g_ref[...])), wq_ref[...], preferred_element_type=F32)
    dh = x.shape[1] // heads
    scale = dh ** -0.5
    outs = []
    for h in range(heads):
        cs = slice(h * dh, (h + 1) * dh)
        sc = _dot_nt(hq[:, cs], k_ref[0, :, cs]) * scale
        e = jnp.exp(sc - jnp.max(sc, axis=-1, keepdims=True))
        prob = e / jnp.sum(e, axis=-1, keepdims=True)
        outs.append(_dot(prob, v_ref[0, :, cs]))
    att = jnp.concatenate(outs, axis=-1)
    o_ref[0] = x + jnp.dot(_mx(att), wo_ref[...], preferred_element_type=F32)


def _xattn(x, g, wq, mk, mv, wo):
    B, S, Dm = x.shape
    Mt = mk.shape[1]
    tile = min(S, XA_TILE)
    assert S % tile == 0
    return pl.pallas_call(
        functools.partial(_xattn_kernel, heads=HEADS),
        grid=(B, S // tile),
        in_specs=[pl.BlockSpec((1, tile, Dm), lambda b, s: (b, s, 0)), _full_spec(g.shape), _full_spec(wq.shape),
                  pl.BlockSpec((1, Mt, Dm), lambda b, s: (b, 0, 0)), pl.BlockSpec((1, Mt, Dm), lambda b, s: (b, 0, 0)),
                  _full_spec(wo.shape)],
        out_specs=pl.BlockSpec((1, tile, Dm), lambda b, s: (b, s, 0)),
        out_shape=jax.ShapeDtypeStruct((B, S, Dm), F32),
        compiler_params=_cparams("parallel", "parallel"),
        name="xattn",
    )(x, g, wq, mk, mv, wo)


def _ffn_kernel(x_ref, g_ref, wu_ref, wd_ref, gf_ref, o_ref, *, final):
    x = x_ref[...]
    hn = _mx(_rms(x, g_ref[...]))
    acc = x
    dff = wu_ref.shape[1]
    fc = min(dff, FFN_CHUNK)
    for c in range(dff // fc):
        up = jnp.dot(hn, wu_ref[:, c * fc:(c + 1) * fc], preferred_element_type=F32)
        act = jnp.square(jnp.maximum(up, 0.0))
        acc = acc + jnp.dot(_mx(act), wd_ref[c * fc:(c + 1) * fc, :], preferred_element_type=F32)
    o_ref[...] = _rms(acc, gf_ref[...]) if final else acc


def _ffn(x2d, g, wu, wd, gf, final):
    M, Dm = x2d.shape
    tile = min(M, FFN_TILE)
    assert M % tile == 0
    return pl.pallas_call(
        functools.partial(_ffn_kernel, final=final),
        grid=(M // tile,),
        in_specs=[pl.BlockSpec((tile, Dm), lambda i: (i, 0)), _full_spec(g.shape), _full_spec(wu.shape),
                  _full_spec(wd.shape), _full_spec(gf.shape)],
        out_specs=pl.BlockSpec((tile, Dm), lambda i: (i, 0)),
        out_shape=jax.ShapeDtypeStruct((M, Dm), F32),
        compiler_params=_cparams("parallel"),
        name="ffn",
    )(x2d, g, wu, wd, gf)


def _s_front_kernel(x_ref, gmix_ref, win_ref, wg_ref, gbrow_ref, gmvg_ref, ws0_ref, bs0_ref, cw_ref, cb_ref,
                    cst_ref, wq_ref, wk_ref, wv_ref, wqt_ref, wkt_ref, n0_ref, m0_ref, skip_ref,
                    ygm_ref, gmv_ref, cnew_ref, qt_ref, kt_ref, vs_ref, a_ref, bc_ref, nn_ref, mt_ref, wi_ref,
                    og_ref, sc_ref, *, width):
    W = width
    D = HEAD_DIM
    x = x_ref[...]
    hn = _mx(_rms(x, gmix_ref[...]))
    proj = jnp.dot(hn, win_ref[...], preferred_element_type=F32)
    g_col = jnp.dot(hn, wg_ref[...], preferred_element_type=F32) + gbrow_ref[...]
    ig = g_col[:, :LANES]
    lf = _log_sigmoid(g_col[:, LANES:])

    z = jax.nn.gelu(proj[:, :2 * W])
    v_gm = _rms(z[:, W:], gmvg_ref[...])
    gmv_ref[...] = v_gm
    ygm_ref[...] = z[:, :W] * (ws0_ref[...] * v_gm + bs0_ref[...])

    xm = proj[:, 2 * W:3 * W]
    conv = cb_ref[...] + cw_ref[ML_CONV - 1:ML_CONV, :] * xm
    for j in range(ML_CONV - 1):
        conv = conv + cw_ref[j:j + 1, :] * cst_ref[:, j * W:(j + 1) * W]
    ca = conv * jax.nn.sigmoid(conv)
    cnew_ref[:, :(ML_CONV - 2) * W] = cst_ref[:, W:]
    cnew_ref[:, (ML_CONV - 2) * W:] = xm
    og_ref[...] = jax.nn.sigmoid(proj[:, 3 * W:])
    sc_ref[...] = skip_ref[...] * ca

    a = lf + m0_ref[...]
    mt = jnp.maximum(a, ig)
    w_inter = jnp.exp(a - mt)
    w_in_gate = jnp.exp(ig - mt)
    floor = jnp.exp(-mt)
    mt_ref[...] = mt
    wi_ref[...] = w_inter
    k_scale = D ** -0.5
    for h in range(HEADS):
        cs = slice(h * D, (h + 1) * D)
        ca_h = _mx(ca[:, cs])
        q = jnp.dot(ca_h, wq_ref[h], preferred_element_type=F32)
        k = jnp.dot(ca_h, wk_ref[h], preferred_element_type=F32) * k_scale
        v = jnp.dot(_mx(xm[:, cs]), wv_ref[h], preferred_element_type=F32)
        qt_ref[h] = _dot_nt(wqt_ref[h], ca_h)
        kt_ref[h] = _dot_nt(wkt_ref[h], ca_h) * k_scale
        wi_h = w_inter[:, h:h + 1]
        wg_h = w_in_gate[:, h:h + 1]
        n0 = n0_ref[:, cs]
        s = jnp.sum(q * k, axis=1, keepdims=True) * wg_h
        den = s + wi_h * jnp.sum(q * n0, axis=1, keepdims=True)
        dnm = jnp.maximum(jnp.abs(den), floor[:, h:h + 1])
        a_ref[:, cs] = s * v / dnm
        bc_ref[:, cs] = jnp.broadcast_to(wi_h / dnm, (x.shape[0], D))
        vs_ref[:, cs] = wg_h * v
        nn_ref[:, cs] = wi_h * n0 + wg_h * k


def _s_front(x, cst, n0, m0p, p):
    N, Dm = x.shape
    W = HEADS * HEAD_DIM
    ins = (x, p["g_mix"], p["w_in_main"], p["w_g"], p["gb_row"], p["gm_v_g"], p["gm_ws0"], p["gm_bs0"],
           p["conv_w"], p["conv_b"], cst, p["wq"], p["wk"], p["wv"], p["wqt"], p["wkt"], n0, m0p, p["skip"])
    row = jax.ShapeDtypeStruct((N, W), F32)
    colw = jax.ShapeDtypeStruct((N, LANES), F32)
    tr = jax.ShapeDtypeStruct((HEADS, HEAD_DIM, N), F32)
    outs = [row, row, jax.ShapeDtypeStruct((N, (ML_CONV - 1) * W), F32), tr, tr, row, row, row, row, colw, colw, row, row]
    return pl.pallas_call(
        functools.partial(_s_front_kernel, width=W),
        grid=(1,),
        in_specs=[_full_spec(a.shape) for a in ins],
        out_specs=[_full_spec(o.shape) for o in outs],
        out_shape=outs,
        compiler_params=_cparams("arbitrary"),
        name="s_front",
    )(*ins)


def _s_state_kernel(dec_ref, c_ref, qt_ref, kt_ref, vs_ref, cn_ref, qc_ref, *, block, nseq):
    D = HEAD_DIM
    base = pl.program_id(0) * block
    shift = (nseq - base) % nseq
    for h in range(HEADS):
        qt = pltpu.roll(qt_ref[h], shift, 1)
        kt = pltpu.roll(kt_ref[h], shift, 1)
        for j in range(block):
            C = c_ref[j, h]
            qc_ref[j:j + 1, h * D:(h + 1) * D] = jnp.sum(C * qt[:, j:j + 1], axis=0, keepdims=True)
            decay = dec_ref[(base + j) * HEADS + h]
            cn_ref[j, h] = decay * C + kt[:, j:j + 1] * vs_ref[j:j + 1, h * D:(h + 1) * D]


def _s_state(dec, C, qt, kt, vs):
    N = C.shape[0]
    W = HEADS * HEAD_DIM
    bb = min(N, STATE_BLOCK)
    assert N % bb == 0
    cblk = pl.BlockSpec((bb, HEADS, HEAD_DIM, HEAD_DIM), lambda i: (i, 0, 0, 0))
    rblk = pl.BlockSpec((bb, W), lambda i: (i, 0))
    return pl.pallas_call(
        functools.partial(_s_state_kernel, block=bb, nseq=N),
        grid=(N // bb,),
        in_specs=[pl.BlockSpec(memory_space=pltpu.SMEM), cblk, _full_spec(qt.shape), _full_spec(kt.shape), rblk],
        out_specs=[cblk, rblk],
        out_shape=[jax.ShapeDtypeStruct(C.shape, F32), jax.ShapeDtypeStruct((N, W), F32)],
        compiler_params=_cparams("parallel"),
        name="s_state",
    )(dec, C, qt, kt, vs)


def _s_back_kernel(x_ref, ygm_ref, a_ref, bc_ref, qc_ref, og_ref, sc_ref, outg_ref, wout_ref, gxa_ref, wcq_ref,
                   x1_ref, hq_ref, *, width):
    W = width
    D = HEAD_DIM
    hc = a_ref[...] + bc_ref[...] * qc_ref[...]
    parts = [ygm_ref[...]]
    for h in range(HEADS):
        cs = slice(h * D, (h + 1) * D)
        parts.append(og_ref[:, cs] * (_rms(hc[:, cs], outg_ref[:, cs]) + sc_ref[:, cs]))
    y = jnp.concatenate(parts, axis=-1)
    x1 = x_ref[...] + jnp.dot(_mx(y), wout_ref[...], preferred_element_type=F32)
    x1_ref[...] = x1
    hq_ref[...] = jnp.dot(_mx(_rms(x1, gxa_ref[...])), wcq_ref[...], preferred_element_type=F32)


def _s_back(x, ygm, a, bc, qc, og, sc, p):
    N, Dm = x.shape
    ins = (x, ygm, a, bc, qc, og, sc, p["out_g"], p["w_out"], p["g_xa"], p["w_cq"])
    out = jax.ShapeDtypeStruct((N, Dm), F32)
    return pl.pallas_call(
        functools.partial(_s_back_kernel, width=HEADS * HEAD_DIM),
        grid=(1,),
        in_specs=[_full_spec(a_.shape) for a_ in ins],
        out_specs=[_full_spec(out.shape)] * 2,
        out_shape=[out, out],
        compiler_params=_cparams("arbitrary"),
        name="s_back",
    )(*ins)


def _s_attn_kernel(hq_ref, k_ref, v_ref, o_ref, *, block, heads):
    dh = hq_ref.shape[2] // heads
    scale = dh ** -0.5
    for j in range(block):
        for h in range(heads):
            cs = slice(h * dh, (h + 1) * dh)
            q = hq_ref[0, j:j + 1, cs]
            sc = jnp.sum(k_ref[j, :, cs] * q, axis=1, keepdims=True) * scale
            e = jnp.exp(sc - jnp.max(sc, axis=0, keepdims=True))
            prob = e / jnp.sum(e, axis=0, keepdims=True)
            o_ref[0, j:j + 1, cs] = jnp.sum(v_ref[j, :, cs] * prob, axis=0, keepdims=True)


def _s_attn(hq, K, V):
    N, Dm = hq.shape
    Mt = K.shape[1]
    bb = min(N, STATE_BLOCK)
    assert N % bb == 0
    hq3 = hq.reshape(N // bb, bb, Dm)
    kv = pl.BlockSpec((bb, Mt, Dm), lambda i: (i, 0, 0))
    out = pl.pallas_call(
        functools.partial(_s_attn_kernel, block=bb, heads=HEADS),
        grid=(N // bb,),
        in_specs=[pl.BlockSpec((1, bb, Dm), lambda i: (i, 0, 0)), kv, kv],
        out_specs=pl.BlockSpec((1, bb, Dm), lambda i: (i, 0, 0)),
        out_shape=jax.ShapeDtypeStruct((N // bb, bb, Dm), F32),
        compiler_params=_cparams("parallel"),
        name="s_attn",
    )(hq3, K, V)
    return out.reshape(N, Dm)


def _s_post_kernel(x1_ref, att_ref, wco_ref, gffn_ref, wu_ref, wd_ref, gf_ref, o_ref, hn_s, *, final):
    i = pl.program_id(0)

    @pl.when(i == 0)
    def _first():
        x2 = x1_ref[...] + jnp.dot(_mx(att_ref[...]), wco_ref[...], preferred_element_type=F32)
        o_ref[...] = x2
        hn_s[...] = _rms(x2, gffn_ref[...])

    up = jnp.dot(_mx(hn_s[...]), wu_ref[...], preferred_element_type=F32)
    act = jnp.square(jnp.maximum(up, 0.0))
    o_ref[...] += jnp.dot(_mx(act), wd_ref[...], preferred_element_type=F32)

    if final:
        @pl.when(i == pl.num_programs(0) - 1)
        def _last():
            o_ref[...] = _rms(o_ref[...], gf_ref[...])


def _s_post(x1, att, p, gf, final):
    N, Dm = x1.shape
    dff = p["w_up"].shape[1]
    fc = min(dff, FFN_CHUNK)
    return pl.pallas_call(
        functools.partial(_s_post_kernel, final=final),
        grid=(dff // fc,),
        in_specs=[_full_spec((N, Dm)), _full_spec((N, Dm)), _full_spec(p["w_co"].shape), _full_spec((1, Dm)),
                  pl.BlockSpec((Dm, fc), lambda i: (0, i)), pl.BlockSpec((fc, Dm), lambda i: (i, 0)),
                  _full_spec((1, Dm))],
        out_specs=_full_spec((N, Dm)),
        out_shape=jax.ShapeDtypeStruct((N, Dm), F32),
        scratch_shapes=[pltpu.VMEM((N, Dm), F32)],
        compiler_params=_cparams("arbitrary"),
        name="s_post",
    )(x1, att, p["w_co"], p["g_ffn"], p["w_up"], p["w_down"], gf)


def _layer_params(l, norm_mix_g, w_in, gm_v_norm_g, gm_ws, gm_bs, ml_conv_w, ml_conv_b, ml_wq, ml_wk, ml_wv,
                  ml_b_i, ml_b_f, ml_out_norm_g, ml_skip, w_out, norm_mem_g, w_ck, w_cv, norm_xa_g, w_cq, w_co,
                  norm_ffn_g, w_up, w_down):
    W = HEADS * HEAD_DIM
    Dm = w_in.shape[1]
    row = lambda a: a.reshape(1, -1).astype(F32)
    wi = w_in[l]
    w_gate = wi[:, 4 * W:]
    w_g = jnp.zeros((Dm, 2 * LANES), F32).at[:, :HEADS].set(w_gate[:, :HEADS]).at[:, LANES:LANES + HEADS].set(w_gate[:, HEADS:])
    w_gt = jnp.zeros((2 * SUBLANES, Dm), F32).at[:HEADS].set(w_gate[:, :HEADS].T).at[SUBLANES:SUBLANES + HEADS].set(w_gate[:, HEADS:].T)
    gb_row = jnp.zeros((1, 2 * LANES), F32).at[0, :HEADS].set(ml_b_i[l]).at[0, LANES:LANES + HEADS].set(ml_b_f[l])
    gb_col = jnp.zeros((2 * SUBLANES, 1), F32).at[:HEADS, 0].set(ml_b_i[l]).at[SUBLANES:SUBLANES + HEADS, 0].set(ml_b_f[l])
    return {
        "g_mix": row(norm_mix_g[l]), "w_in_main": _mx(wi[:, :4 * W]), "w_g": _mx(w_g), "w_gt": _mx(w_gt),
        "gb_row": gb_row, "gb_col": gb_col, "gm_v_g": row(gm_v_norm_g[l]), "gm_ws": gm_ws[l], "gm_bst": gm_bs[l].T,
        "gm_ws0": row(jnp.repeat(gm_ws[l][:, 0, 0], HEAD_DIM)), "gm_bs0": row(jnp.repeat(gm_bs[l][:, 0], HEAD_DIM)),
        "conv_w": ml_conv_w[l], "conv_b": row(ml_conv_b[l]),
        "wq": _mx(ml_wq[l]), "wk": _mx(ml_wk[l]), "wv": _mx(ml_wv[l]),
        "wqt": _mx(jnp.swapaxes(ml_wq[l], 1, 2)), "wkt": _mx(jnp.swapaxes(ml_wk[l], 1, 2)),
        "out_g": row(ml_out_norm_g[l]), "skip": row(ml_skip[l]), "w_out": _mx(w_out[l]),
        "g_mem": row(norm_mem_g[l]), "w_ck": _mx(w_ck[l]), "w_cv": _mx(w_cv[l]),
        "g_xa": row(norm_xa_g[l]), "w_cq": _mx(w_cq[l]), "w_co": _mx(w_co[l]),
        "g_ffn": row(norm_ffn_g[l]), "w_up": _mx(w_up[l]), "w_down": _mx(w_down[l]),
    }


def kernel(x_prompt, x_sample, mem_prompt, cache_mem_k, cache_mem_v, state_C, state_n, state_m, state_conv, norm_mix_g, w_in, gm_v_norm_g, gm_ws, gm_bs, ml_conv_w, ml_conv_b, ml_wq, ml_wk, ml_wv, ml_b_i, ml_b_f, ml_out_norm_g, ml_skip, w_out, norm_mem_g, w_ck, w_cv, norm_xa_g, w_cq, w_co, norm_ffn_g, w_up, w_down, norm_f_g):
    depth = w_in.shape[0]
    B, S, Dm = x_prompt.shape
    N = x_sample.shape[0]
    assert x_sample.shape[1] == 1
    Mt = mem_prompt.shape[1]
    W = HEADS * HEAD_DIM
    xa_dh = Dm // HEADS
    gf = norm_f_g.reshape(1, Dm).astype(F32)

    xp = x_prompt
    xs = x_sample.reshape(N, Dm)
    mem2d = mem_prompt.reshape(B * Mt, Dm)
    mk_p, mv_p, C_p, n_p, m_p, cv_p = [], [], [], [], [], []
    C_s, n_s, m_s, cv_s, gv_s = [], [], [], [], []
    for l in range(depth):
        p = _layer_params(l, norm_mix_g, w_in, gm_v_norm_g, gm_ws, gm_bs, ml_conv_w, ml_conv_b, ml_wq, ml_wk, ml_wv,
                          ml_b_i, ml_b_f, ml_out_norm_g, ml_skip, w_out, norm_mem_g, w_ck, w_cv, norm_xa_g, w_cq,
                          w_co, norm_ffn_g, w_up, w_down)
        final = l == depth - 1

        mk, mv = _memkv(mem2d, p["g_mem"], p["w_ck"], p["w_cv"])
        xp, Cp, np_, mp, cvp = _mix(xp, p)
        xp = _xattn(xp, p["g_xa"], p["w_cq"], mk.reshape(B, Mt, Dm), mv.reshape(B, Mt, Dm), p["w_co"])
        xp = _ffn(xp.reshape(B * S, Dm), p["g_ffn"], p["w_up"], p["w_down"], gf, final).reshape(B, S, Dm)
        mk_p.append(mk.reshape(B, Mt, HEADS, xa_dh))
        mv_p.append(mv.reshape(B, Mt, HEADS, xa_dh))
        C_p.append(Cp)
        n_p.append(np_)
        m_p.append(mp[:, :, 0])
        cv_p.append(cvp)

        m0p = jnp.zeros((N, LANES), F32).at[:, :HEADS].set(state_m[l])
        (ygm, gmv, cnew, qt, kt, vs, a, bc, nn, mt, wi, og, sc) = _s_front(
            xs, state_conv[l].reshape(N, (ML_CONV - 1) * W), state_n[l].reshape(N, W), m0p, p)
        Cn, qc = _s_state(wi[:, :HEADS].reshape(N * HEADS), state_C[l], qt, kt, vs)
        x1, hq = _s_back(xs, ygm, a, bc, qc, og, sc, p)
        att = _s_attn(hq, cache_mem_k[l].reshape(N, Mt, Dm), cache_mem_v[l].reshape(N, Mt, Dm))
        xs = _s_post(x1, att, p, gf, final)
        C_s.append(Cn)
        n_s.append(nn.reshape(N, HEADS, HEAD_DIM))
        m_s.append(mt[:, :HEADS])
        cv_s.append(cnew.reshape(N, ML_CONV - 1, W))
        gv_s.append(gmv.reshape(N, 1, W))

    st = lambda a: jnp.stack(a, axis=0)
    return (xp, xs.reshape(N, 1, Dm), st(mk_p), st(mv_p), st(C_p), st(n_p), st(m_p), st(cv_p),
            st(C_s), st(n_s), st(m_s), st(cv_s), st(gv_s))
```

```python
import functools

import jax
import jax.numpy as jnp
from jax import lax
from jax.experimental import pallas as pl
from jax.experimental.pallas import tpu as pltpu

F32 = jnp.float32
MXU_DTYPE = jnp.bfloat16

EPS = 1e-6
NEG = -1e30
LANES = 128
SUBLANES = 8
HEADS = 4
HEAD_DIM = 128
GM_CHUNK = 128
ML_CHUNK = 128
ML_CONV = 4
MIX_TILE = 256
XA_TILE = 256
FFN_TILE = 512
FFN_CHUNK = 1024
STATE_BLOCK = 8
VMEM_LIMIT = 56 * 1024 * 1024


def _cparams(*sem):
    return pltpu.CompilerParams(dimension_semantics=sem, vmem_limit_bytes=VMEM_LIMIT)


def _mx(a):
    return a.astype(MXU_DTYPE)


def _dot(a, b):
    return jnp.dot(_mx(a), _mx(b), preferred_element_type=F32)


def _dot_nt(a, b):
    return lax.dot_general(_mx(a), _mx(b), (((1,), (1,)), ((), ())), preferred_element_type=F32)


def _rms(x, g):
    return x * lax.rsqrt(jnp.mean(x * x, axis=-1, keepdims=True) + EPS) * g


def _log_sigmoid(x):
    return jnp.minimum(x, 0.0) - jnp.log1p(jnp.exp(-jnp.abs(x)))


def _split3(x):
    hi = x.astype(MXU_DTYPE)
    r1 = x - hi.astype(F32)
    mid = r1.astype(MXU_DTYPE)
    lo = (r1 - mid.astype(F32)).astype(MXU_DTYPE)
    return hi, mid, lo


def _full_spec(shape):
    n = len(shape)
    return pl.BlockSpec(shape, lambda *_: (0,) * n)


def _memkv_kernel(m_ref, g_ref, wk_ref, wv_ref, k_ref, v_ref):
    hn = _mx(_rms(m_ref[...], g_ref[0]))
    k_ref[0] = jnp.dot(hn, wk_ref[0], preferred_element_type=F32)
    v_ref[0] = jnp.dot(hn, wv_ref[0], preferred_element_type=F32)


def _memkv(mem2d, g, wk, wv):
    M, D = mem2d.shape
    depth = wk.shape[0]
    tm = min(M, 512)
    wspec = pl.BlockSpec((1, D, D), lambda l, i: (l, 0, 0))
    ospec = pl.BlockSpec((1, tm, D), lambda l, i: (l, i, 0))
    return pl.pallas_call(
        _memkv_kernel,
        grid=(depth, M // tm),
        in_specs=[pl.BlockSpec((tm, D), lambda l, i: (i, 0)), pl.BlockSpec((1, 1, D), lambda l, i: (l, 0, 0)), wspec, wspec],
        out_specs=[ospec, ospec],
        out_shape=[jax.ShapeDtypeStruct((depth, M, D), F32)] * 2,
        compiler_params=_cparams("parallel", "parallel"),
        name="memkv",
    )(mem2d, g, wk, wv)


def _mix_kernel(x_ref, gmix_ref, win_ref, wg_ref, wgt_ref, gbrow_ref, gbcol_ref, gmvg_ref, ws_ref, bst_ref,
                cw_ref, cb_ref, wq_ref, wk_ref, wv_ref, outg_ref, skip_ref, wout_ref,
                xo_ref, c_ref, n_ref, m_ref, conv_ref, xe_s, y_s, *, tile, width):
    W = width
    L = ML_CHUNK
    D = HEAD_DIM

    @pl.when(pl.program_id(1) == 0)
    def _init():
        xe_s[0:SUBLANES, :] = jnp.zeros((SUBLANES, W), F32)
        c_ref[...] = jnp.zeros_like(c_ref)
        n_ref[...] = jnp.zeros_like(n_ref)
        m_ref[...] = jnp.zeros_like(m_ref)

    x = x_ref[0]
    hn = _mx(_rms(x, gmix_ref[...]))
    proj = jnp.dot(hn, win_ref[...], preferred_element_type=F32)
    g_col = jnp.dot(hn, wg_ref[...], preferred_element_type=F32) + gbrow_ref[...]
    g_row = _dot_nt(wgt_ref[...], hn) + gbcol_ref[...]
    ig_col = g_col[:, :LANES]
    lf_col = _log_sigmoid(g_col[:, LANES:])
    ig_row = g_row[:SUBLANES]
    lf_row = _log_sigmoid(g_row[SUBLANES:])

    z = jax.nn.gelu(proj[:, :2 * W])
    u = z[:, :W]
    v_gm = _rms(z[:, W:], gmvg_ref[...])
    row = lax.broadcasted_iota(jnp.int32, (GM_CHUNK, GM_CHUNK), 0)
    col = lax.broadcasted_iota(jnp.int32, (GM_CHUNK, GM_CHUNK), 1)
    causal = col <= row
    for h in range(HEADS):
        w_h = _mx(jnp.where(causal, ws_ref[h], 0.0))
        b_h = bst_ref[:, h:h + 1]
        for c in range(tile // GM_CHUNK):
            rs = slice(c * GM_CHUNK, (c + 1) * GM_CHUNK)
            cs = slice(h * D, (h + 1) * D)
            mixed = jnp.dot(w_h, _mx(v_gm[rs, cs]), preferred_element_type=F32) + b_h
            y_s[rs, cs] = u[rs, cs] * mixed

    xm = proj[:, 2 * W:3 * W]
    o_gate = jax.nn.sigmoid(proj[:, 3 * W:])
    xe_s[SUBLANES:SUBLANES + tile, :] = xm
    conv = cb_ref[...] + cw_ref[ML_CONV - 1:ML_CONV, :] * xm
    for j in range(1, ML_CONV):
        conv = conv + cw_ref[ML_CONV - 1 - j:ML_CONV - j, :] * xe_s[SUBLANES - j:SUBLANES - j + tile, :]
    ca = conv * jax.nn.sigmoid(conv)
    conv_ref[0] = xe_s[SUBLANES + tile - (ML_CONV - 1):SUBLANES + tile, :]
    xe_s[0:SUBLANES, :] = xe_s[tile:tile + SUBLANES, :]

    tril = _mx(jnp.where(col <= row, 1.0, 0.0))
    triu = _mx(jnp.where(row <= col, 1.0, 0.0))
    k_scale = D ** -0.5
    for c in range(tile // L):
        rs = slice(c * L, (c + 1) * L)
        b_cols = sum(jnp.dot(tril, p, preferred_element_type=F32) for p in _split3(lf_col[rs]))
        b_rows = sum(jnp.dot(p, triu, preferred_element_type=F32) for p in _split3(lf_row[:, rs]))
        for h in range(HEADS):
            cs = slice(h * D, (h + 1) * D)
            ca_h = _mx(ca[rs, cs])
            q = jnp.dot(ca_h, wq_ref[h], preferred_element_type=F32)
            k = jnp.dot(ca_h, wk_ref[h], preferred_element_type=F32) * k_scale
            v = jnp.dot(_mx(xm[rs, cs]), wv_ref[h], preferred_element_type=F32)
            b_c = b_cols[:, h:h + 1]
            i_c = ig_col[rs, h:h + 1]
            b_r = b_rows[h:h + 1, :]
            i_r = ig_row[h:h + 1, rs]
            m_prev = m_ref[0, h:h + 1, 0:1]
            C = c_ref[0, h]
            n_row = n_ref[0, h:h + 1, :]

            a_c = b_c + m_prev
            dlog = jnp.where(causal, b_c - b_r + i_r, NEG)
            mt = jnp.maximum(a_c, jnp.max(dlog, axis=1, keepdims=True))
            w_inter = jnp.exp(a_c - mt)
            s = _dot_nt(q, k) * jnp.exp(dlog - mt)
            num = _dot(s, v) + w_inter * _dot(q, C)
            den = jnp.sum(s, axis=1, keepdims=True) + w_inter * jnp.sum(q * n_row, axis=1, keepdims=True)
            hc = num / jnp.maximum(jnp.abs(den), jnp.exp(-mt))

            b_last = b_c[L - 1:L, :]
            a_last = b_last + m_prev
            wlog = b_last - b_c + i_c
            m_new = jnp.maximum(a_last, jnp.max(wlog, axis=0, keepdims=True))
            kw = k * jnp.exp(wlog - m_new)
            decay = jnp.exp(a_last - m_new)
            c_ref[0, h] = decay * C + _dot(jnp.transpose(kw), v)
            n_ref[0, h:h + 1, :] = decay * n_row + jnp.sum(kw, axis=0, keepdims=True)
            m_ref[0, h:h + 1, :] = jnp.broadcast_to(m_new, (1, LANES))

            hcn = _rms(hc, outg_ref[:, cs])
            y_s[rs, W + h * D:W + (h + 1) * D] = o_gate[rs, cs] * (hcn + skip_ref[:, cs] * ca[rs, cs])

    xo_ref[0] = x + jnp.dot(_mx(y_s[...]), wout_ref[...], preferred_element_type=F32)


def _mix(x, p):
    B, S, Dm = x.shape
    W = HEADS * HEAD_DIM
    tile = min(S, MIX_TILE)
    assert S % tile == 0 and tile % GM_CHUNK == 0 and tile % ML_CHUNK == 0
    weights = (p["g_mix"], p["w_in_main"], p["w_g"], p["w_gt"], p["gb_row"], p["gb_col"], p["gm_v_g"], p["gm_ws"],
               p["gm_bst"], p["conv_w"], p["conv_b"], p["wq"], p["wk"], p["wv"], p["out_g"], p["skip"], p["w_out"])
    return pl.pallas_call(
        functools.partial(_mix_kernel, tile=tile, width=W),
        grid=(B, S // tile),
        in_specs=[pl.BlockSpec((1, tile, Dm), lambda b, s: (b, s, 0))] + [_full_spec(w.shape) for w in weights],
        out_specs=[pl.BlockSpec((1, tile, Dm), lambda b, s: (b, s, 0)),
                   pl.BlockSpec((1, HEADS, HEAD_DIM, HEAD_DIM), lambda b, s: (b, 0, 0, 0)),
                   pl.BlockSpec((1, HEADS, HEAD_DIM), lambda b, s: (b, 0, 0)),
                   pl.BlockSpec((1, HEADS, LANES), lambda b, s: (b, 0, 0)),
                   pl.BlockSpec((1, ML_CONV - 1, W), lambda b, s: (b, 0, 0))],
        out_shape=[jax.ShapeDtypeStruct((B, S, Dm), F32),
                   jax.ShapeDtypeStruct((B, HEADS, HEAD_DIM, HEAD_DIM), F32),
                   jax.ShapeDtypeStruct((B, HEADS, HEAD_DIM), F32),
                   jax.ShapeDtypeStruct((B, HEADS, LANES), F32),
                   jax.ShapeDtypeStruct((B, ML_CONV - 1, W), F32)],
        scratch_shapes=[pltpu.VMEM((tile + SUBLANES, W), F32), pltpu.VMEM((tile, 2 * W), F32)],
        compiler_params=_cparams("parallel", "arbitrary"),
        name="mix",
    )(x, *weights)


def _xattn_kernel(x_ref, g_ref, wq_ref, k_ref, v_ref, wo_ref, o_ref, *, heads):
    x = x_ref[0]
    hq = jnp.dot(_mx(_rms(x, g_ref[...])), wq_ref[...], preferred_element_type=F32)
    dh = x.shape[1] // heads
    scale = dh ** -0.5
    outs = []
    for h in range(heads):
        cs = slice(h * dh, (h + 1) * dh)
        sc = _dot_nt(hq[:, cs], k_ref[0, :, cs]) * scale
        e = jnp.exp(sc - jnp.max(sc, axis=-1, keepdims=True))
        prob = e / jnp.sum(e, axis=-1, keepdims=True)
        outs.append(_dot(prob, v_ref[0, :, cs]))
    att = jnp.concatenate(outs, axis=-1)
    o_ref[0] = x + jnp.dot(_mx(att), wo_ref[...], preferred_element_type=F32)


def _xattn(x, g, wq, mk, mv, wo, layer):
    B, S, Dm = x.shape
    Mt = mk.shape[1]
    kv = pl.BlockSpec((1, Mt, Dm), lambda b, s: (layer * B + b, 0, 0))
    tile = min(S, XA_TILE)
    assert S % tile == 0
    return pl.pallas_call(
        functools.partial(_xattn_kernel, heads=HEADS),
        grid=(B, S // tile),
        in_specs=[pl.BlockSpec((1, tile, Dm), lambda b, s: (b, s, 0)), _full_spec(g.shape), _full_spec(wq.shape),
                  kv, kv, _full_spec(wo.shape)],
        out_specs=pl.BlockSpec((1, tile, Dm), lambda b, s: (b, s, 0)),
        out_shape=jax.ShapeDtypeStruct((B, S, Dm), F32),
        compiler_params=_cparams("parallel", "parallel"),
        name="xattn",
    )(x, g, wq, mk, mv, wo)


def _ffn_kernel(x_ref, g_ref, wu_ref, wd_ref, gf_ref, o_ref, *, final):
    x = x_ref[...]
    hn = _mx(_rms(x, g_ref[...]))
    acc = x
    dff = wu_ref.shape[1]
    fc = min(dff, FFN_CHUNK)
    for c in range(dff // fc):
        up = jnp.dot(hn, wu_ref[:, c * fc:(c + 1) * fc], preferred_element_type=F32)
        act = jnp.square(jnp.maximum(up, 0.0))
        acc = acc + jnp.dot(_mx(act), wd_ref[c * fc:(c + 1) * fc, :], preferred_element_type=F32)
    o_ref[...] = _rms(acc, gf_ref[...]) if final else acc


def _ffn(x2d, g, wu, wd, gf, final):
    M, Dm = x2d.shape
    tile = min(M, FFN_TILE)
    assert M % tile == 0
    return pl.pallas_call(
        functools.partial(_ffn_kernel, final=final),
        grid=(M // tile,),
        in_specs=[pl.BlockSpec((tile, Dm), lambda i: (i, 0)), _full_spec(g.shape), _full_spec(wu.shape),
                  _full_spec(wd.shape), _full_spec(gf.shape)],
        out_specs=pl.BlockSpec((tile, Dm), lambda i: (i, 0)),
        out_shape=jax.ShapeDtypeStruct((M, Dm), F32),
        compiler_params=_cparams("parallel"),
        name="ffn",
    )(x2d, g, wu, wd, gf)


def _s_front_kernel(x_ref, gmix_ref, win_ref, wg_ref, gbrow_ref, gmvg_ref, ws0_ref, bs0_ref, cw_ref, cb_ref,
                    cst_ref, wq_ref, wk_ref, wv_ref, wqt_ref, wkt_ref, n0_ref, m0_ref, skip_ref,
                    ygm_ref, gmv_ref, cnew_ref, qt_ref, kt_ref, vs_ref, a_ref, bc_ref, nn_ref, mt_ref, wi_ref,
                    og_ref, sc_ref, *, width):
    W = width
    D = HEAD_DIM
    x = x_ref[...]
    hn = _mx(_rms(x, gmix_ref[...]))
    proj = jnp.dot(hn, win_ref[...], preferred_element_type=F32)
    g_col = jnp.dot(hn, wg_ref[...], preferred_element_type=F32) + gbrow_ref[...]
    ig = g_col[:, :LANES]
    lf = _log_sigmoid(g_col[:, LANES:])

    z = jax.nn.gelu(proj[:, :2 * W])
    v_gm = _rms(z[:, W:], gmvg_ref[...])
    gmv_ref[...] = v_gm
    ygm_ref[...] = z[:, :W] * (ws0_ref[...] * v_gm + bs0_ref[...])

    xm = proj[:, 2 * W:3 * W]
    conv = cb_ref[...] + cw_ref[ML_CONV - 1:ML_CONV, :] * xm
    for j in range(ML_CONV - 1):
        conv = conv + cw_ref[j:j + 1, :] * cst_ref[:, j * W:(j + 1) * W]
    ca = conv * jax.nn.sigmoid(conv)
    cnew_ref[:, :(ML_CONV - 2) * W] = cst_ref[:, W:]
    cnew_ref[:, (ML_CONV - 2) * W:] = xm
    og_ref[...] = jax.nn.sigmoid(proj[:, 3 * W:])
    sc_ref[...] = skip_ref[...] * ca

    a = lf + m0_ref[...]
    mt = jnp.maximum(a, ig)
    w_inter = jnp.exp(a - mt)
    w_in_gate = jnp.exp(ig - mt)
    floor = jnp.exp(-mt)
    mt_ref[...] = mt
    wi_ref[...] = w_inter
    k_scale = D ** -0.5
    for h in range(HEADS):
        cs = slice(h * D, (h + 1) * D)
        ca_h = _mx(ca[:, cs])
        q = jnp.dot(ca_h, wq_ref[h], preferred_element_type=F32)
        k = jnp.dot(ca_h, wk_ref[h], preferred_element_type=F32) * k_scale
        v = jnp.dot(_mx(xm[:, cs]), wv_ref[h], preferred_element_type=F32)
        qt_ref[h] = _dot_nt(wqt_ref[h], ca_h)
        kt_ref[h] = _dot_nt(wkt_ref[h], ca_h) * k_scale
        wi_h = w_inter[:, h:h + 1]
        wg_h = w_in_gate[:, h:h + 1]
        n0 = n0_ref[:, cs]
        s = jnp.sum(q * k, axis=1, keepdims=True) * wg_h
        den = s + wi_h * jnp.sum(q * n0, axis=1, keepdims=True)
        dnm = jnp.maximum(jnp.abs(den), floor[:, h:h + 1])
        a_ref[:, cs] = s * v / dnm
        bc_ref[:, cs] = jnp.broadcast_to(wi_h / dnm, (x.shape[0], D))
        vs_ref[:, cs] = wg_h * v
        nn_ref[:, cs] = wi_h * n0 + wg_h * k


def _s_front(x, cst, n0, m0p, p):
    N, Dm = x.shape
    W = HEADS * HEAD_DIM
    ins = (x, p["g_mix"], p["w_in_main"], p["w_g"], p["gb_row"], p["gm_v_g"], p["gm_ws0"], p["gm_bs0"],
           p["conv_w"], p["conv_b"], cst, p["wq"], p["wk"], p["wv"], p["wqt"], p["wkt"], n0, m0p, p["skip"])
    row = jax.ShapeDtypeStruct((N, W), F32)
    colw = jax.ShapeDtypeStruct((N, LANES), F32)
    tr = jax.ShapeDtypeStruct((HEADS, HEAD_DIM, N), F32)
    outs = [row, row, jax.ShapeDtypeStruct((N, (ML_CONV - 1) * W), F32), tr, tr, row, row, row, row, colw, colw, row, row]
    return pl.pallas_call(
        functools.partial(_s_front_kernel, width=W),
        grid=(1,),
        in_specs=[_full_spec(a.shape) for a in ins],
        out_specs=[_full_spec(o.shape) for o in outs],
        out_shape=outs,
        compiler_params=_cparams("arbitrary"),
        name="s_front",
    )(*ins)


def _s_state_kernel(dec_ref, c_ref, qt_ref, kt_ref, vs_ref, *rest, block, nseq):
    cn_ref, qc_ref = rest[-2:]
    D = HEAD_DIM
    base = pl.program_id(0) * block
    shift = (nseq - base) % nseq
    for h in range(HEADS):
        qt = pltpu.roll(qt_ref[h], shift, 1)
        kt = pltpu.roll(kt_ref[h], shift, 1)
        for j in range(block):
            C = c_ref[0, j, h]
            qc_ref[j:j + 1, h * D:(h + 1) * D] = jnp.sum(C * qt[:, j:j + 1], axis=0, keepdims=True)
            decay = dec_ref[(base + j) * HEADS + h]
            cn_ref[0, j, h] = decay * C + kt[:, j:j + 1] * vs_ref[j:j + 1, h * D:(h + 1) * D]


def _s_state(dec, C_all, qt, kt, vs, layer, C_new_all):
    depth, N = C_all.shape[:2]
    W = HEADS * HEAD_DIM
    bb = min(N, STATE_BLOCK)
    assert N % bb == 0
    cblk = pl.BlockSpec((1, bb, HEADS, HEAD_DIM, HEAD_DIM), lambda i: (layer, i, 0, 0, 0))
    rblk = pl.BlockSpec((bb, W), lambda i: (i, 0))
    ins = [dec, C_all, qt, kt, vs]
    in_specs = [pl.BlockSpec(memory_space=pltpu.SMEM), cblk, _full_spec(qt.shape), _full_spec(kt.shape), rblk]
    aliases = {}
    if C_new_all is not None:
        ins.append(C_new_all)
        in_specs.append(pl.BlockSpec(memory_space=pl.ANY))
        aliases = {len(ins) - 1: 0}
    return pl.pallas_call(
        functools.partial(_s_state_kernel, block=bb, nseq=N),
        grid=(N // bb,),
        in_specs=in_specs,
        out_specs=[cblk, rblk],
        out_shape=[jax.ShapeDtypeStruct(C_all.shape, F32), jax.ShapeDtypeStruct((N, W), F32)],
        input_output_aliases=aliases,
        compiler_params=_cparams("parallel"),
        name="s_state",
    )(*ins)


def _s_back_kernel(x_ref, ygm_ref, a_ref, bc_ref, qc_ref, og_ref, sc_ref, outg_ref, wout_ref, gxa_ref, wcq_ref,
                   x1_ref, hq_ref, *, width):
    W = width
    D = HEAD_DIM
    hc = a_ref[...] + bc_ref[...] * qc_ref[...]
    parts = [ygm_ref[...]]
    for h in range(HEADS):
        cs = slice(h * D, (h + 1) * D)
        parts.append(og_ref[:, cs] * (_rms(hc[:, cs], outg_ref[:, cs]) + sc_ref[:, cs]))
    y = jnp.concatenate(parts, axis=-1)
    x1 = x_ref[...] + jnp.dot(_mx(y), wout_ref[...], preferred_element_type=F32)
    x1_ref[...] = x1
    hq_ref[...] = jnp.dot(_mx(_rms(x1, gxa_ref[...])), wcq_ref[...], preferred_element_type=F32)


def _s_back(x, ygm, a, bc, qc, og, sc, p):
    N, Dm = x.shape
    ins = (x, ygm, a, bc, qc, og, sc, p["out_g"], p["w_out"], p["g_xa"], p["w_cq"])
    out = jax.ShapeDtypeStruct((N, Dm), F32)
    return pl.pallas_call(
        functools.partial(_s_back_kernel, width=HEADS * HEAD_DIM),
        grid=(1,),
        in_specs=[_full_spec(a_.shape) for a_ in ins],
        out_specs=[_full_spec(out.shape)] * 2,
        out_shape=[out, out],
        compiler_params=_cparams("arbitrary"),
        name="s_back",
    )(*ins)


def _s_attn_kernel(hq_ref, k_ref, v_ref, o_ref, *, block):
    scale = hq_ref.shape[-1] ** -0.5
    for j in range(block):
        q = hq_ref[0, j]
        sc = jnp.sum(k_ref[0, j] * q[None], axis=-1, keepdims=True) * scale
        e = jnp.exp(sc - jnp.max(sc, axis=0, keepdims=True))
        prob = e / jnp.sum(e, axis=0, keepdims=True)
        o_ref[0, j] = jnp.sum(v_ref[0, j] * prob, axis=0)


def _s_attn(hq, K_all, V_all, layer):
    N, Dm = hq.shape
    _, _, Mt, H, dh = K_all.shape
    bb = min(N, STATE_BLOCK)
    assert N % bb == 0
    kv = pl.BlockSpec((1, bb, Mt, H, dh), lambda i: (layer, i, 0, 0, 0))
    qo = pl.BlockSpec((1, bb, H, dh), lambda i: (i, 0, 0, 0))
    out = pl.pallas_call(
        functools.partial(_s_attn_kernel, block=bb),
        grid=(N // bb,),
        in_specs=[qo, kv, kv],
        out_specs=qo,
        out_shape=jax.ShapeDtypeStruct((N // bb, bb, H, dh), F32),
        compiler_params=_cparams("parallel"),
        name="s_attn",
    )(hq.reshape(N // bb, bb, H, dh), K_all, V_all)
    return out.reshape(N, Dm)


def _s_post_kernel(x1_ref, att_ref, wco_ref, gffn_ref, wu_ref, wd_ref, gf_ref, o_ref, hn_s, *, final):
    i = pl.program_id(0)

    @pl.when(i == 0)
    def _first():
        x2 = x1_ref[...] + jnp.dot(_mx(att_ref[...]), wco_ref[...], preferred_element_type=F32)
        o_ref[...] = x2
        hn_s[...] = _rms(x2, gffn_ref[...])

    up = jnp.dot(_mx(hn_s[...]), wu_ref[...], preferred_element_type=F32)
    act = jnp.square(jnp.maximum(up, 0.0))
    o_ref[...] += jnp.dot(_mx(act), wd_ref[...], preferred_element_type=F32)

    if final:
        @pl.when(i == pl.num_programs(0) - 1)
        def _last():
            o_ref[...] = _rms(o_ref[...], gf_ref[...])


def _s_post(x1, att, p, gf, final):
    N, Dm = x1.shape
    dff = p["w_up"].shape[1]
    fc = min(dff, FFN_CHUNK)
    return pl.pallas_call(
        functools.partial(_s_post_kernel, final=final),
        grid=(dff // fc,),
        in_specs=[_full_spec((N, Dm)), _full_spec((N, Dm)), _full_spec(p["w_co"].shape), _full_spec((1, Dm)),
                  pl.BlockSpec((Dm, fc), lambda i: (0, i)), pl.BlockSpec((fc, Dm), lambda i: (i, 0)),
                  _full_spec((1, Dm))],
        out_specs=_full_spec((N, Dm)),
        out_shape=jax.ShapeDtypeStruct((N, Dm), F32),
        scratch_shapes=[pltpu.VMEM((N, Dm), F32)],
        compiler_params=_cparams("arbitrary"),
        name="s_post",
    )(x1, att, p["w_co"], p["g_ffn"], p["w_up"], p["w_down"], gf)


def _layer_params(l, norm_mix_g, w_in, gm_v_norm_g, gm_ws, gm_bs, ml_conv_w, ml_conv_b, ml_wq, ml_wk, ml_wv,
                  ml_b_i, ml_b_f, ml_out_norm_g, ml_skip, w_out, norm_mem_g, w_ck, w_cv, norm_xa_g, w_cq, w_co,
                  norm_ffn_g, w_up, w_down):
    W = HEADS * HEAD_DIM
    Dm = w_in.shape[1]
    row = lambda a: a.reshape(1, -1).astype(F32)
    wi = w_in[l]
    w_gate = wi[:, 4 * W:]
    w_g = jnp.zeros((Dm, 2 * LANES), F32).at[:, :HEADS].set(w_gate[:, :HEADS]).at[:, LANES:LANES + HEADS].set(w_gate[:, HEADS:])
    w_gt = jnp.zeros((2 * SUBLANES, Dm), F32).at[:HEADS].set(w_gate[:, :HEADS].T).at[SUBLANES:SUBLANES + HEADS].set(w_gate[:, HEADS:].T)
    gb_row = jnp.zeros((1, 2 * LANES), F32).at[0, :HEADS].set(ml_b_i[l]).at[0, LANES:LANES + HEADS].set(ml_b_f[l])
    gb_col = jnp.zeros((2 * SUBLANES, 1), F32).at[:HEADS, 0].set(ml_b_i[l]).at[SUBLANES:SUBLANES + HEADS, 0].set(ml_b_f[l])
    return {
        "g_mix": row(norm_mix_g[l]), "w_in_main": _mx(wi[:, :4 * W]), "w_g": _mx(w_g), "w_gt": _mx(w_gt),
        "gb_row": gb_row, "gb_col": gb_col, "gm_v_g": row(gm_v_norm_g[l]), "gm_ws": gm_ws[l], "gm_bst": gm_bs[l].T,
        "gm_ws0": row(jnp.repeat(gm_ws[l][:, 0, 0], HEAD_DIM)), "gm_bs0": row(jnp.repeat(gm_bs[l][:, 0], HEAD_DIM)),
        "conv_w": ml_conv_w[l], "conv_b": row(ml_conv_b[l]),
        "wq": _mx(ml_wq[l]), "wk": _mx(ml_wk[l]), "wv": _mx(ml_wv[l]),
        "wqt": _mx(jnp.swapaxes(ml_wq[l], 1, 2)), "wkt": _mx(jnp.swapaxes(ml_wk[l], 1, 2)),
        "out_g": row(ml_out_norm_g[l]), "skip": row(ml_skip[l]), "w_out": _mx(w_out[l]),
        "g_xa": row(norm_xa_g[l]), "w_cq": _mx(w_cq[l]), "w_co": _mx(w_co[l]),
        "g_ffn": row(norm_ffn_g[l]), "w_up": _mx(w_up[l]), "w_down": _mx(w_down[l]),
    }


def kernel(x_prompt, x_sample, mem_prompt, cache_mem_k, cache_mem_v, state_C, state_n, state_m, state_conv, norm_mix_g, w_in, gm_v_norm_g, gm_ws, gm_bs, ml_conv_w, ml_conv_b, ml_wq, ml_wk, ml_wv, ml_b_i, ml_b_f, ml_out_norm_g, ml_skip, w_out, norm_mem_g, w_ck, w_cv, norm_xa_g, w_cq, w_co, norm_ffn_g, w_up, w_down, norm_f_g):
    depth = w_in.shape[0]
    B, S, Dm = x_prompt.shape
    N = x_sample.shape[0]
    assert x_sample.shape[1] == 1
    Mt = mem_prompt.shape[1]
    W = HEADS * HEAD_DIM
    xa_dh = Dm // HEADS
    gf = norm_f_g.reshape(1, Dm).astype(F32)

    xp = x_prompt
    xs = x_sample.reshape(N, Dm)
    mem2d = mem_prompt.reshape(B * Mt, Dm)
    mk_all, mv_all = _memkv(mem2d, norm_mem_g.reshape(depth, 1, Dm), _mx(w_ck), _mx(w_cv))
    mk_rows = mk_all.reshape(depth * B, Mt, Dm)
    mv_rows = mv_all.reshape(depth * B, Mt, Dm)
    C_p, n_p, m_p, cv_p = [], [], [], []
    n_s, m_s, cv_s, gv_s = [], [], [], []
    C_s = None
    for l in range(depth):
        p = _layer_params(l, norm_mix_g, w_in, gm_v_norm_g, gm_ws, gm_bs, ml_conv_w, ml_conv_b, ml_wq, ml_wk, ml_wv,
                          ml_b_i, ml_b_f, ml_out_norm_g, ml_skip, w_out, norm_mem_g, w_ck, w_cv, norm_xa_g, w_cq,
                          w_co, norm_ffn_g, w_up, w_down)
        final = l == depth - 1

        xp, Cp, np_, mp, cvp = _mix(xp, p)
        xp = _xattn(xp, p["g_xa"], p["w_cq"], mk_rows, mv_rows, p["w_co"], l)
        xp = _ffn(xp.reshape(B * S, Dm), p["g_ffn"], p["w_up"], p["w_down"], gf, final).reshape(B, S, Dm)
        C_p.append(Cp)
        n_p.append(np_)
        m_p.append(mp[:, :, 0])
        cv_p.append(cvp)

        m0p = jnp.zeros((N, LANES), F32).at[:, :HEADS].set(state_m[l])
        (ygm, gmv, cnew, qt, kt, vs, a, bc, nn, mt, wi, og, sc) = _s_front(
            xs, state_conv[l].reshape(N, (ML_CONV - 1) * W), state_n[l].reshape(N, W), m0p, p)
        C_s, qc = _s_state(wi[:, :HEADS].reshape(N * HEADS), state_C, qt, kt, vs, l, C_s)
        x1, hq = _s_back(xs, ygm, a, bc, qc, og, sc, p)
        att = _s_attn(hq, cache_mem_k, cache_mem_v, l)
        xs = _s_post(x1, att, p, gf, final)
        n_s.append(nn.reshape(N, HEADS, HEAD_DIM))
        m_s.append(mt[:, :HEADS])
        cv_s.append(cnew.reshape(N, ML_CONV - 1, W))
        gv_s.append(gmv.reshape(N, 1, W))

    st = lambda a: jnp.stack(a, axis=0)
    return (xp, xs.reshape(N, 1, Dm), mk_all.reshape(depth, B, Mt, HEADS, xa_dh), mv_all.reshape(depth, B, Mt, HEADS, xa_dh),
            st(C_p), st(n_p), st(m_p), st(cv_p), C_s, st(n_s), st(m_s), st(cv_s), st(gv_s))
```

```python
import functools

import jax
import jax.numpy as jnp
from jax import lax
from jax.experimental import pallas as pl
from jax.experimental.pallas import tpu as pltpu

F32 = jnp.float32
MXU_DTYPE = jnp.bfloat16

EPS = 1e-6
NEG = -1e30
LANES = 128
SUBLANES = 8
HEADS = 4
HEAD_DIM = 128
GM_CHUNK = 128
ML_CHUNK = 256
ML_CONV = 4
MIX_TILE = 512
XA_TILE = 256
FFN_TILE = 512
FFN_CHUNK = 1024
STATE_BLOCK = 8
VMEM_LIMIT = 56 * 1024 * 1024


def _cparams(*sem):
    return pltpu.CompilerParams(dimension_semantics=sem, vmem_limit_bytes=VMEM_LIMIT)


def _mx(a):
    return a.astype(MXU_DTYPE)


def _dot(a, b):
    return jnp.dot(_mx(a), _mx(b), preferred_element_type=F32)


def _dot_nt(a, b):
    return lax.dot_general(_mx(a), _mx(b), (((1,), (1,)), ((), ())), preferred_element_type=F32)


def _rms(x, g):
    return x * lax.rsqrt(jnp.mean(x * x, axis=-1, keepdims=True) + EPS) * g


def _log_sigmoid(x):
    return jnp.minimum(x, 0.0) - jnp.log1p(jnp.exp(-jnp.abs(x)))


def _split3(x):
    hi = x.astype(MXU_DTYPE)
    r1 = x - hi.astype(F32)
    mid = r1.astype(MXU_DTYPE)
    lo = (r1 - mid.astype(F32)).astype(MXU_DTYPE)
    return hi, mid, lo


def _full_spec(shape):
    n = len(shape)
    return pl.BlockSpec(shape, lambda *_: (0,) * n)


def _memkv_kernel(m_ref, g_ref, wk_ref, wv_ref, k_ref, v_ref):
    hn = _mx(_rms(m_ref[...], g_ref[0]))
    k_ref[0] = jnp.dot(hn, wk_ref[0], preferred_element_type=F32)
    v_ref[0] = jnp.dot(hn, wv_ref[0], preferred_element_type=F32)


def _memkv(mem2d, g, wk, wv):
    M, D = mem2d.shape
    depth = wk.shape[0]
    tm = min(M, 512)
    wspec = pl.BlockSpec((1, D, D), lambda l, i: (l, 0, 0))
    ospec = pl.BlockSpec((1, tm, D), lambda l, i: (l, i, 0))
    return pl.pallas_call(
        _memkv_kernel,
        grid=(depth, M // tm),
        in_specs=[pl.BlockSpec((tm, D), lambda l, i: (i, 0)), pl.BlockSpec((1, 1, D), lambda l, i: (l, 0, 0)), wspec, wspec],
        out_specs=[ospec, ospec],
        out_shape=[jax.ShapeDtypeStruct((depth, M, D), F32)] * 2,
        compiler_params=_cparams("parallel", "parallel"),
        name="memkv",
    )(mem2d, g, wk, wv)


def _transpose_exact(eye, rows):
    return sum(lax.dot_general(eye, p, (((1,), (1,)), ((), ())), preferred_element_type=F32) for p in _split3(rows))


def _mix_kernel(x_ref, gmix_ref, win_ref, wgt_ref, gbcol_ref, gmvg_ref, ws_ref, bst_ref,
                cw_ref, cb_ref, wqk_ref, wv_ref, outg_ref, skip_ref, wout_ref,
                xo_ref, c_ref, n_ref, m_ref, conv_ref, xe_s, ygm_s, yml_s, ca_s, m_s, *, tile, width):
    W = width
    L = min(tile, ML_CHUNK)
    D = HEAD_DIM
    R = SUBLANES

    @pl.when(pl.program_id(1) == 0)
    def _init():
        xe_s[0:SUBLANES, :] = jnp.zeros((SUBLANES, W), F32)
        ca_s[...] = jnp.zeros_like(ca_s)
        m_s[...] = jnp.zeros_like(m_s)

    x = x_ref[0]
    hn = _mx(_rms(x, gmix_ref[...]))
    g_row = _dot_nt(wgt_ref[...], hn) + gbcol_ref[...]
    ig_row = g_row[:R]
    lf_row = _log_sigmoid(g_row[R:])

    row_i = lax.broadcasted_iota(jnp.int32, (L, L), 0)
    col_i = lax.broadcasted_iota(jnp.int32, (L, L), 1)
    causal = col_i <= row_i
    triu = _mx(jnp.where(row_i <= col_i, 1.0, 0.0))
    eye = _mx(jnp.where(row_i == col_i, 1.0, 0.0))
    n_ch = tile // L
    chunks = [slice(c * L, (c + 1) * L) for c in range(n_ch)]
    b_all = jnp.concatenate([sum(jnp.dot(p, triu, preferred_element_type=F32) for p in _split3(lf_row[:, rs]))
                             for rs in chunks], axis=1)
    g_all = ig_row - b_all
    pos = lax.broadcasted_iota(jnp.int32, (R, tile), 1) % L

    NB = 2 * D
    n_blk = 4 * W // NB
    proj_blk = []
    run_max = g_all
    step = 1
    for i in range(n_blk):
        proj_blk.append(jnp.dot(hn, win_ref[:, i * NB:(i + 1) * NB], preferred_element_type=F32))
        if step < L:
            run_max = jnp.maximum(run_max, jnp.where(pos >= step, pltpu.roll(run_max, step, 1), NEG))
            step *= 2
    while step < L:
        run_max = jnp.maximum(run_max, jnp.where(pos >= step, pltpu.roll(run_max, step, 1), NEG))
        step *= 2
    proj = jnp.concatenate(proj_blk, axis=1)

    g_r, cols, decay_all = [], [], []
    m_prev = m_s[:, 0:1]
    for rs in chunks:
        b_r = b_all[:, rs]
        g = g_all[:, rs]
        M_r = jnp.maximum(run_max[:, rs], m_prev)
        M_last = M_r[:, L - 1:L]
        g_r.append(g)
        cols.append(_transpose_exact(eye, jnp.concatenate(
            [M_r, jnp.exp(m_prev - M_r), jnp.exp(-(b_r + M_r)), jnp.exp(g - M_last)], axis=0)))
        decay_all.append(jnp.exp(m_prev - M_last))
        m_prev = b_r[:, L - 1:L] + M_last
    m_s[...] = jnp.broadcast_to(m_prev, (R, LANES))

    xm = proj[:, :W]
    xe_s[SUBLANES:SUBLANES + tile, :] = xm
    conv = cb_ref[...] + cw_ref[ML_CONV - 1:ML_CONV, :] * xm
    for j in range(1, ML_CONV):
        conv = conv + cw_ref[ML_CONV - 1 - j:ML_CONV - j, :] * xe_s[SUBLANES - j:SUBLANES - j + tile, :]
    ca = conv * jax.nn.sigmoid(conv)
    conv_ref[0] = xe_s[SUBLANES + tile - (ML_CONV - 1):SUBLANES + tile, :]
    xe_s[0:SUBLANES, :] = xe_s[tile:tile + SUBLANES, :]
    ones_blk = jnp.ones((tile, D), F32)
    k_scale = D ** -0.5
    q, k, va = [], [], []
    for h in range(HEADS):
        cs = slice(h * D, (h + 1) * D)
        qk = jnp.dot(_mx(ca[:, cs]), wqk_ref[h], preferred_element_type=F32)
        q.append(_mx(qk[:, :D]))
        k.append(qk[:, D:] * k_scale)
        v = jnp.dot(_mx(xm[:, cs]), wv_ref[h], preferred_element_type=F32)
        va.append(_mx(jnp.concatenate([v, ones_blk], axis=1)))

    z = jax.nn.gelu(proj[:, W:3 * W])
    u = z[:, :W]
    v_gm = _rms(z[:, W:], gmvg_ref[...])
    G = GM_CHUNK
    g_causal = lax.broadcasted_iota(jnp.int32, (G, G), 1) <= lax.broadcasted_iota(jnp.int32, (G, G), 0)
    n_gm = tile // G
    for h in range(HEADS):
        cs = slice(h * D, (h + 1) * D)
        w_h = _mx(jnp.where(g_causal, ws_ref[h], 0.0))
        b_h = bst_ref[:, h:h + 1]
        for c0 in range(0, n_gm, 2):
            cc = list(range(c0, min(c0 + 2, n_gm)))
            vv = jnp.concatenate([_mx(v_gm[c * G:(c + 1) * G, cs]) for c in cc], axis=1)
            mixed = jnp.dot(w_h, vv, preferred_element_type=F32) + b_h
            for i, c in enumerate(cc):
                rs = slice(c * G, (c + 1) * G)
                ygm_s[rs, cs] = u[rs, cs] * mixed[:, i * D:(i + 1) * D]
    acc = x + jnp.dot(_mx(ygm_s[...]), wout_ref[0:W, :], preferred_element_type=F32)

    pairs = [(c, h) for c in range(n_ch) for h in range(HEADS)]

    def scores(c, h):
        return _dot_nt(q[h][chunks[c]], k[h][chunks[c]])

    Ca = [ca_s[h] for h in range(HEADS)]
    s_next = scores(*pairs[0])
    for i, (c, h) in enumerate(pairs):
        rs = chunks[c]
        cs = slice(h * D, (h + 1) * D)
        s_raw = s_next
        if i + 1 < len(pairs):
            s_next = scores(*pairs[i + 1])
        decay_mat = jnp.exp(jnp.where(causal, g_r[c][h:h + 1, :] - cols[c][:, h:h + 1], NEG))
        sv = jnp.dot(_mx(s_raw * decay_mat), va[h][rs], preferred_element_type=F32)
        qc = jnp.dot(q[h][rs], _mx(Ca[h]), preferred_element_type=F32)
        w_inter = cols[c][:, R + h:R + h + 1]
        floor = cols[c][:, 2 * R + h:2 * R + h + 1]
        w_s = cols[c][:, 3 * R + h:3 * R + h + 1]
        Ca[h] = decay_all[c][h:h + 1, :] * Ca[h] + jnp.dot(_mx(jnp.transpose(k[h][rs] * w_s)), va[h][rs],
                                                           preferred_element_type=F32)
        num = sv[:, :D] + w_inter * qc[:, :D]
        den = sv[:, D:] + w_inter * qc[:, D:]
        hc = num / jnp.maximum(jnp.abs(den), floor)
        hcn = _rms(hc, outg_ref[:, cs])
        yml_s[rs, cs] = jax.nn.sigmoid(proj[rs, 3 * W + h * D:3 * W + (h + 1) * D]) * (hcn + skip_ref[:, cs] * ca[rs, cs])
    for h in range(HEADS):
        ca_s[h] = Ca[h]

    xo_ref[0] = acc + jnp.dot(_mx(yml_s[...]), wout_ref[W:2 * W, :], preferred_element_type=F32)

    @pl.when(pl.program_id(1) == pl.num_programs(1) - 1)
    def _emit_state():
        for h in range(HEADS):
            c_ref[0, h] = ca_s[h, :, :D]
            n_ref[0, h:h + 1, :] = jnp.transpose(ca_s[h, :, D:])[0:1, :]
        m_ref[0] = m_s[0:HEADS, :]


def _mix(x, p):
    B, S, Dm = x.shape
    W = HEADS * HEAD_DIM
    tile = min(S, MIX_TILE)
    assert S % tile == 0 and tile % GM_CHUNK == 0 and tile % min(tile, ML_CHUNK) == 0
    weights = (p["g_mix"], p["w_in_main"], p["w_gt"], p["gb_col"], p["gm_v_g"], p["gm_ws"],
               p["gm_bst"], p["conv_w"], p["conv_b"], p["wqk"], p["wv"], p["out_g"], p["skip"], p["w_out"])
    return pl.pallas_call(
        functools.partial(_mix_kernel, tile=tile, width=W),
        grid=(B, S // tile),
        in_specs=[pl.BlockSpec((1, tile, Dm), lambda b, s: (b, s, 0))] + [_full_spec(w.shape) for w in weights],
        out_specs=[pl.BlockSpec((1, tile, Dm), lambda b, s: (b, s, 0)),
                   pl.BlockSpec((1, HEADS, HEAD_DIM, HEAD_DIM), lambda b, s: (b, 0, 0, 0)),
                   pl.BlockSpec((1, HEADS, HEAD_DIM), lambda b, s: (b, 0, 0)),
                   pl.BlockSpec((1, HEADS, LANES), lambda b, s: (b, 0, 0)),
                   pl.BlockSpec((1, ML_CONV - 1, W), lambda b, s: (b, 0, 0))],
        out_shape=[jax.ShapeDtypeStruct((B, S, Dm), F32),
                   jax.ShapeDtypeStruct((B, HEADS, HEAD_DIM, HEAD_DIM), F32),
                   jax.ShapeDtypeStruct((B, HEADS, HEAD_DIM), F32),
                   jax.ShapeDtypeStruct((B, HEADS, LANES), F32),
                   jax.ShapeDtypeStruct((B, ML_CONV - 1, W), F32)],
        scratch_shapes=[pltpu.VMEM((tile + SUBLANES, W), F32), pltpu.VMEM((tile, W), F32), pltpu.VMEM((tile, W), F32),
                        pltpu.VMEM((HEADS, HEAD_DIM, 2 * HEAD_DIM), F32), pltpu.VMEM((SUBLANES, LANES), F32)],
        compiler_params=_cparams("parallel", "arbitrary"),
        name="mix",
    )(x, *weights)


def _xattn_kernel(x_ref, g_ref, wq_ref, k_ref, v_ref, wo_ref, o_ref, *, heads):
    x = x_ref[0]
    hq = jnp.dot(_mx(_rms(x, g_ref[...])), wq_ref[...], preferred_element_type=F32)
    dh = x.shape[1] // heads
    scale = dh ** -0.5
    outs = []
    for h in range(heads):
        cs = slice(h * dh, (h + 1) * dh)
        sc = _dot_nt(hq[:, cs], k_ref[0, :, cs]) * scale
        e = jnp.exp(sc - jnp.max(sc, axis=-1, keepdims=True))
        prob = e / jnp.sum(e, axis=-1, keepdims=True)
        outs.append(_dot(prob, v_ref[0, :, cs]))
    att = jnp.concatenate(outs, axis=-1)
    o_ref[0] = x + jnp.dot(_mx(att), wo_ref[...], preferred_element_type=F32)


def _xattn(x, g, wq, mk, mv, wo, layer):
    B, S, Dm = x.shape
    Mt = mk.shape[1]
    kv = pl.BlockSpec((1, Mt, Dm), lambda b, s: (layer * B + b, 0, 0))
    tile = min(S, XA_TILE)
    assert S % tile == 0
    return pl.pallas_call(
        functools.partial(_xattn_kernel, heads=HEADS),
        grid=(B, S // tile),
        in_specs=[pl.BlockSpec((1, tile, Dm), lambda b, s: (b, s, 0)), _full_spec(g.shape), _full_spec(wq.shape),
                  kv, kv, _full_spec(wo.shape)],
        out_specs=pl.BlockSpec((1, tile, Dm), lambda b, s: (b, s, 0)),
        out_shape=jax.ShapeDtypeStruct((B, S, Dm), F32),
        compiler_params=_cparams("parallel", "parallel"),
        name="xattn",
    )(x, g, wq, mk, mv, wo)


def _ffn_kernel(x_ref, g_ref, wu_ref, wd_ref, gf_ref, o_ref, *, final):
    x = x_ref[...]
    hn = _mx(_rms(x, g_ref[...]))
    acc = x
    dff = wu_ref.shape[1]
    fc = min(dff, FFN_CHUNK)
    for c in range(dff // fc):
        up = jnp.dot(hn, wu_ref[:, c * fc:(c + 1) * fc], preferred_element_type=F32)
        act = jnp.square(jnp.maximum(up, 0.0))
        acc = acc + jnp.dot(_mx(act), wd_ref[c * fc:(c + 1) * fc, :], preferred_element_type=F32)
    o_ref[...] = _rms(acc, gf_ref[...]) if final else acc


def _ffn(x2d, g, wu, wd, gf, final):
    M, Dm = x2d.shape
    tile = min(M, FFN_TILE)
    assert M % tile == 0
    return pl.pallas_call(
        functools.partial(_ffn_kernel, final=final),
        grid=(M // tile,),
        in_specs=[pl.BlockSpec((tile, Dm), lambda i: (i, 0)), _full_spec(g.shape), _full_spec(wu.shape),
                  _full_spec(wd.shape), _full_spec(gf.shape)],
        out_specs=pl.BlockSpec((tile, Dm), lambda i: (i, 0)),
        out_shape=jax.ShapeDtypeStruct((M, Dm), F32),
        compiler_params=_cparams("parallel"),
        name="ffn",
    )(x2d, g, wu, wd, gf)


def _s_front_kernel(x_ref, gmix_ref, win_ref, wg_ref, gbrow_ref, gmvg_ref, ws0_ref, bs0_ref, cw_ref, cb_ref,
                    cst_ref, wq_ref, wk_ref, wv_ref, wqt_ref, wkt_ref, n0_ref, m0_ref, skip_ref,
                    ygm_ref, gmv_ref, cnew_ref, qt_ref, kt_ref, vs_ref, a_ref, bc_ref, nn_ref, mt_ref, wi_ref,
                    og_ref, sc_ref, *, width):
    W = width
    D = HEAD_DIM
    x = x_ref[...]
    hn = _mx(_rms(x, gmix_ref[...]))
    proj = jnp.dot(hn, win_ref[...], preferred_element_type=F32)
    g_col = jnp.dot(hn, wg_ref[...], preferred_element_type=F32) + gbrow_ref[...]
    ig = g_col[:, :LANES]
    lf = _log_sigmoid(g_col[:, LANES:])

    z = jax.nn.gelu(proj[:, W:3 * W])
    v_gm = _rms(z[:, W:], gmvg_ref[...])
    gmv_ref[...] = v_gm
    ygm_ref[...] = z[:, :W] * (ws0_ref[...] * v_gm + bs0_ref[...])

    xm = proj[:, :W]
    conv = cb_ref[...] + cw_ref[ML_CONV - 1:ML_CONV, :] * xm
    for j in range(ML_CONV - 1):
        conv = conv + cw_ref[j:j + 1, :] * cst_ref[:, j * W:(j + 1) * W]
    ca = conv * jax.nn.sigmoid(conv)
    cnew_ref[:, :(ML_CONV - 2) * W] = cst_ref[:, W:]
    cnew_ref[:, (ML_CONV - 2) * W:] = xm
    og_ref[...] = jax.nn.sigmoid(proj[:, 3 * W:])
    sc_ref[...] = skip_ref[...] * ca

    a = lf + m0_ref[...]
    mt = jnp.maximum(a, ig)
    w_inter = jnp.exp(a - mt)
    w_in_gate = jnp.exp(ig - mt)
    floor = jnp.exp(-mt)
    mt_ref[...] = mt
    wi_ref[...] = w_inter
    k_scale = D ** -0.5
    for h in range(HEADS):
        cs = slice(h * D, (h + 1) * D)
        ca_h = _mx(ca[:, cs])
        q = jnp.dot(ca_h, wq_ref[h], preferred_element_type=F32)
        k = jnp.dot(ca_h, wk_ref[h], preferred_element_type=F32) * k_scale
        v = jnp.dot(_mx(xm[:, cs]), wv_ref[h], preferred_element_type=F32)
        qt_ref[h] = _dot_nt(wqt_ref[h], ca_h)
        kt_ref[h] = _dot_nt(wkt_ref[h], ca_h) * k_scale
        wi_h = w_inter[:, h:h + 1]
        wg_h = w_in_gate[:, h:h + 1]
        n0 = n0_ref[:, cs]
        s = jnp.sum(q * k, axis=1, keepdims=True) * wg_h
        den = s + wi_h * jnp.sum(q * n0, axis=1, keepdims=True)
        dnm = jnp.maximum(jnp.abs(den), floor[:, h:h + 1])
        a_ref[:, cs] = s * v / dnm
        bc_ref[:, cs] = jnp.broadcast_to(wi_h / dnm, (x.shape[0], D))
        vs_ref[:, cs] = wg_h * v
        nn_ref[:, cs] = wi_h * n0 + wg_h * k


def _s_front(x, cst, n0, m0p, p):
    N, Dm = x.shape
    W = HEADS * HEAD_DIM
    ins = (x, p["g_mix"], p["w_in_main"], p["w_g"], p["gb_row"], p["gm_v_g"], p["gm_ws0"], p["gm_bs0"],
           p["conv_w"], p["conv_b"], cst, p["wq"], p["wk"], p["wv"], p["wqt"], p["wkt"], n0, m0p, p["skip"])
    row = jax.ShapeDtypeStruct((N, W), F32)
    colw = jax.ShapeDtypeStruct((N, LANES), F32)
    tr = jax.ShapeDtypeStruct((HEADS, HEAD_DIM, N), F32)
    outs = [row, row, jax.ShapeDtypeStruct((N, (ML_CONV - 1) * W), F32), tr, tr, row, row, row, row, colw, colw, row, row]
    return pl.pallas_call(
        functools.partial(_s_front_kernel, width=W),
        grid=(1,),
        in_specs=[_full_spec(a.shape) for a in ins],
        out_specs=[_full_spec(o.shape) for o in outs],
        out_shape=outs,
        compiler_params=_cparams("arbitrary"),
        name="s_front",
    )(*ins)


def _s_state_kernel(dec_ref, c_ref, qt_ref, kt_ref, vs_ref, *rest, block, nseq):
    cn_ref, qc_ref = rest[-2:]
    D = HEAD_DIM
    base = pl.program_id(0) * block
    shift = (nseq - base) % nseq
    for h in range(HEADS):
        qt = pltpu.roll(qt_ref[h], shift, 1)
        kt = pltpu.roll(kt_ref[h], shift, 1)
        for j in range(block):
            C = c_ref[0, j, h]
            qc_ref[j:j + 1, h * D:(h + 1) * D] = jnp.sum(C * qt[:, j:j + 1], axis=0, keepdims=True)
            decay = dec_ref[(base + j) * HEADS + h]
            cn_ref[0, j, h] = decay * C + kt[:, j:j + 1] * vs_ref[j:j + 1, h * D:(h + 1) * D]


def _s_state(dec, C_all, qt, kt, vs, layer, C_new_all):
    depth, N = C_all.shape[:2]
    W = HEADS * HEAD_DIM
    bb = min(N, STATE_BLOCK)
    assert N % bb == 0
    cblk = pl.BlockSpec((1, bb, HEADS, HEAD_DIM, HEAD_DIM), lambda i: (layer, i, 0, 0, 0))
    rblk = pl.BlockSpec((bb, W), lambda i: (i, 0))
    ins = [dec, C_all, qt, kt, vs]
    in_specs = [pl.BlockSpec(memory_space=pltpu.SMEM), cblk, _full_spec(qt.shape), _full_spec(kt.shape), rblk]
    aliases = {}
    if C_new_all is not None:
        ins.append(C_new_all)
        in_specs.append(pl.BlockSpec(memory_space=pl.ANY))
        aliases = {len(ins) - 1: 0}
    return pl.pallas_call(
        functools.partial(_s_state_kernel, block=bb, nseq=N),
        grid=(N // bb,),
        in_specs=in_specs,
        out_specs=[cblk, rblk],
        out_shape=[jax.ShapeDtypeStruct(C_all.shape, F32), jax.ShapeDtypeStruct((N, W), F32)],
        input_output_aliases=aliases,
        compiler_params=_cparams("parallel"),
        name="s_state",
    )(*ins)


def _s_back_kernel(x_ref, ygm_ref, a_ref, bc_ref, qc_ref, og_ref, sc_ref, outg_ref, wout_ref, gxa_ref, wcq_ref,
                   x1_ref, hq_ref, *, width):
    W = width
    D = HEAD_DIM
    hc = a_ref[...] + bc_ref[...] * qc_ref[...]
    parts = [ygm_ref[...]]
    for h in range(HEADS):
        cs = slice(h * D, (h + 1) * D)
        parts.append(og_ref[:, cs] * (_rms(hc[:, cs], outg_ref[:, cs]) + sc_ref[:, cs]))
    y = jnp.concatenate(parts, axis=-1)
    x1 = x_ref[...] + jnp.dot(_mx(y), wout_ref[...], preferred_element_type=F32)
    x1_ref[...] = x1
    hq_ref[...] = jnp.dot(_mx(_rms(x1, gxa_ref[...])), wcq_ref[...], preferred_element_type=F32)


def _s_back(x, ygm, a, bc, qc, og, sc, p):
    N, Dm = x.shape
    ins = (x, ygm, a, bc, qc, og, sc, p["out_g"], p["w_out"], p["g_xa"], p["w_cq"])
    out = jax.ShapeDtypeStruct((N, Dm), F32)
    return pl.pallas_call(
        functools.partial(_s_back_kernel, width=HEADS * HEAD_DIM),
        grid=(1,),
        in_specs=[_full_spec(a_.shape) for a_ in ins],
        out_specs=[_full_spec(out.shape)] * 2,
        out_shape=[out, out],
        compiler_params=_cparams("arbitrary"),
        name="s_back",
    )(*ins)


def _s_attn_kernel(hq_ref, k_ref, v_ref, o_ref, *, block):
    scale = hq_ref.shape[-1] ** -0.5
    for j in range(block):
        q = hq_ref[0, j]
        sc = jnp.sum(k_ref[0, j] * q[None], axis=-1, keepdims=True) * scale
        e = jnp.exp(sc - jnp.max(sc, axis=0, keepdims=True))
        prob = e / jnp.sum(e, axis=0, keepdims=True)
        o_ref[0, j] = jnp.sum(v_ref[0, j] * prob, axis=0)


def _s_attn(hq, K_all, V_all, layer):
    N, Dm = hq.shape
    _, _, Mt, H, dh = K_all.shape
    bb = min(N, STATE_BLOCK)
    assert N % bb == 0
    kv = pl.BlockSpec((1, bb, Mt, H, dh), lambda i: (layer, i, 0, 0, 0))
    qo = pl.BlockSpec((1, bb, H, dh), lambda i: (i, 0, 0, 0))
    out = pl.pallas_call(
        functools.partial(_s_attn_kernel, block=bb),
        grid=(N // bb,),
        in_specs=[qo, kv, kv],
        out_specs=qo,
        out_shape=jax.ShapeDtypeStruct((N // bb, bb, H, dh), F32),
        compiler_params=_cparams("parallel"),
        name="s_attn",
    )(hq.reshape(N // bb, bb, H, dh), K_all, V_all)
    return out.reshape(N, Dm)


def _s_post_kernel(x1_ref, att_ref, wco_ref, gffn_ref, wu_ref, wd_ref, gf_ref, o_ref, hn_s, *, final):
    i = pl.program_id(0)

    @pl.when(i == 0)
    def _first():
        x2 = x1_ref[...] + jnp.dot(_mx(att_ref[...]), wco_ref[...], preferred_element_type=F32)
        o_ref[...] = x2
        hn_s[...] = _rms(x2, gffn_ref[...])

    up = jnp.dot(_mx(hn_s[...]), wu_ref[...], preferred_element_type=F32)
    act = jnp.square(jnp.maximum(up, 0.0))
    o_ref[...] += jnp.dot(_mx(act), wd_ref[...], preferred_element_type=F32)

    if final:
        @pl.when(i == pl.num_programs(0) - 1)
        def _last():
            o_ref[...] = _rms(o_ref[...], gf_ref[...])


def _s_post(x1, att, p, gf, final):
    N, Dm = x1.shape
    dff = p["w_up"].shape[1]
    fc = min(dff, FFN_CHUNK)
    return pl.pallas_call(
        functools.partial(_s_post_kernel, final=final),
        grid=(dff // fc,),
        in_specs=[_full_spec((N, Dm)), _full_spec((N, Dm)), _full_spec(p["w_co"].shape), _full_spec((1, Dm)),
                  pl.BlockSpec((Dm, fc), lambda i: (0, i)), pl.BlockSpec((fc, Dm), lambda i: (i, 0)),
                  _full_spec((1, Dm))],
        out_specs=_full_spec((N, Dm)),
        out_shape=jax.ShapeDtypeStruct((N, Dm), F32),
        scratch_shapes=[pltpu.VMEM((N, Dm), F32)],
        compiler_params=_cparams("arbitrary"),
        name="s_post",
    )(x1, att, p["w_co"], p["g_ffn"], p["w_up"], p["w_down"], gf)


def _layer_params(l, norm_mix_g, w_in, gm_v_norm_g, gm_ws, gm_bs, ml_conv_w, ml_conv_b, ml_wq, ml_wk, ml_wv,
                  ml_b_i, ml_b_f, ml_out_norm_g, ml_skip, w_out, norm_mem_g, w_ck, w_cv, norm_xa_g, w_cq, w_co,
                  norm_ffn_g, w_up, w_down):
    W = HEADS * HEAD_DIM
    Dm = w_in.shape[1]
    row = lambda a: a.reshape(1, -1).astype(F32)
    wi = w_in[l]
    w_gate = wi[:, 4 * W:]
    w_g = jnp.zeros((Dm, 2 * LANES), F32).at[:, :HEADS].set(w_gate[:, :HEADS]).at[:, LANES:LANES + HEADS].set(w_gate[:, HEADS:])
    w_gt = jnp.zeros((2 * SUBLANES, Dm), F32).at[:HEADS].set(w_gate[:, :HEADS].T).at[SUBLANES:SUBLANES + HEADS].set(w_gate[:, HEADS:].T)
    gb_row = jnp.zeros((1, 2 * LANES), F32).at[0, :HEADS].set(ml_b_i[l]).at[0, LANES:LANES + HEADS].set(ml_b_f[l])
    gb_col = jnp.zeros((2 * SUBLANES, 1), F32).at[:HEADS, 0].set(ml_b_i[l]).at[SUBLANES:SUBLANES + HEADS, 0].set(ml_b_f[l])
    return {
        "g_mix": row(norm_mix_g[l]), "w_in_main": _mx(jnp.concatenate([wi[:, 2 * W:3 * W], wi[:, :2 * W], wi[:, 3 * W:4 * W]], axis=1)), "w_g": _mx(w_g), "w_gt": _mx(w_gt),
        "gb_row": gb_row, "gb_col": gb_col, "gm_v_g": row(gm_v_norm_g[l]), "gm_ws": gm_ws[l], "gm_bst": gm_bs[l].T,
        "gm_ws0": row(jnp.repeat(gm_ws[l][:, 0, 0], HEAD_DIM)), "gm_bs0": row(jnp.repeat(gm_bs[l][:, 0], HEAD_DIM)),
        "conv_w": ml_conv_w[l], "conv_b": row(ml_conv_b[l]),
        "wq": _mx(ml_wq[l]), "wk": _mx(ml_wk[l]), "wv": _mx(ml_wv[l]),
        "wqk": _mx(jnp.concatenate([ml_wq[l], ml_wk[l]], axis=-1)),
        "wqt": _mx(jnp.swapaxes(ml_wq[l], 1, 2)), "wkt": _mx(jnp.swapaxes(ml_wk[l], 1, 2)),
        "out_g": row(ml_out_norm_g[l]), "skip": row(ml_skip[l]), "w_out": _mx(w_out[l]),
        "g_xa": row(norm_xa_g[l]), "w_cq": _mx(w_cq[l]), "w_co": _mx(w_co[l]),
        "g_ffn": row(norm_ffn_g[l]), "w_up": _mx(w_up[l]), "w_down": _mx(w_down[l]),
    }


def kernel(x_prompt, x_sample, mem_prompt, cache_mem_k, cache_mem_v, state_C, state_n, state_m, state_conv, norm_mix_g, w_in, gm_v_norm_g, gm_ws, gm_bs, ml_conv_w, ml_conv_b, ml_wq, ml_wk, ml_wv, ml_b_i, ml_b_f, ml_out_norm_g, ml_skip, w_out, norm_mem_g, w_ck, w_cv, norm_xa_g, w_cq, w_co, norm_ffn_g, w_up, w_down, norm_f_g):
    depth = w_in.shape[0]
    B, S, Dm = x_prompt.shape
    N = x_sample.shape[0]
    assert x_sample.shape[1] == 1
    Mt = mem_prompt.shape[1]
    W = HEADS * HEAD_DIM
    xa_dh = Dm // HEADS
    gf = norm_f_g.reshape(1, Dm).astype(F32)

    xp = x_prompt
    xs = x_sample.reshape(N, Dm)
    mem2d = mem_prompt.reshape(B * Mt, Dm)
    mk_all, mv_all = _memkv(mem2d, norm_mem_g.reshape(depth, 1, Dm), _mx(w_ck), _mx(w_cv))
    mk_rows = mk_all.reshape(depth * B, Mt, Dm)
    mv_rows = mv_all.reshape(depth * B, Mt, Dm)
    C_p, n_p, m_p, cv_p = [], [], [], []
    n_s, m_s, cv_s, gv_s = [], [], [], []
    C_s = None
    for l in range(depth):
        p = _layer_params(l, norm_mix_g, w_in, gm_v_norm_g, gm_ws, gm_bs, ml_conv_w, ml_conv_b, ml_wq, ml_wk, ml_wv,
                          ml_b_i, ml_b_f, ml_out_norm_g, ml_skip, w_out, norm_mem_g, w_ck, w_cv, norm_xa_g, w_cq,
                          w_co, norm_ffn_g, w_up, w_down)
        final = l == depth - 1

        xp, Cp, np_, mp, cvp = _mix(xp, p)
        xp = _xattn(xp, p["g_xa"], p["w_cq"], mk_rows, mv_rows, p["w_co"], l)
        xp = _ffn(xp.reshape(B * S, Dm), p["g_ffn"], p["w_up"], p["w_down"], gf, final).reshape(B, S, Dm)
        C_p.append(Cp)
        n_p.append(np_)
        m_p.append(mp[:, :, 0])
        cv_p.append(cvp)

        m0p = jnp.zeros((N, LANES), F32).at[:, :HEADS].set(state_m[l])
        (ygm, gmv, cnew, qt, kt, vs, a, bc, nn, mt, wi, og, sc) = _s_front(
            xs, state_conv[l].reshape(N, (ML_CONV - 1) * W), state_n[l].reshape(N, W), m0p, p)
        C_s, qc = _s_state(wi[:, :HEADS].reshape(N * HEADS), state_C, qt, kt, vs, l, C_s)
        x1, hq = _s_back(xs, ygm, a, bc, qc, og, sc, p)
        att = _s_attn(hq, cache_mem_k, cache_mem_v, l)
        xs = _s_post(x1, att, p, gf, final)
        n_s.append(nn.reshape(N, HEADS, HEAD_DIM))
        m_s.append(mt[:, :HEADS])
        cv_s.append(cnew.reshape(N, ML_CONV - 1, W))
        gv_s.append(gmv.reshape(N, 1, W))

    st = lambda a: jnp.stack(a, axis=0)
    return (xp, xs.reshape(N, 1, Dm), mk_all.reshape(depth, B, Mt, HEADS, xa_dh), mv_all.reshape(depth, B, Mt, HEADS, xa_dh),
            st(C_p), st(n_p), st(m_p), st(cv_p), C_s, st(n_s), st(m_s), st(cv_s), st(gv_s))
```

```python
import functools

import jax
import jax.numpy as jnp
from jax import lax
from jax.experimental import pallas as pl
from jax.experimental.pallas import tpu as pltpu

F32 = jnp.float32
MXU_DTYPE = jnp.bfloat16

EPS = 1e-6
NEG = -1e30
LANES = 128
SUBLANES = 8
HEADS = 4
HEAD_DIM = 128
GM_CHUNK = 128
ML_CHUNK = 256
ML_CONV = 4
MIX_TILE = 512
XA_TILE = 512
FFN_TILE = 512
FFN_CHUNK = 1024
STATE_BLOCK = 8
VMEM_LIMIT = 56 * 1024 * 1024


def _cparams(*sem):
    return pltpu.CompilerParams(dimension_semantics=sem, vmem_limit_bytes=VMEM_LIMIT)


def _mx(a):
    return a.astype(MXU_DTYPE)


def _dot(a, b):
    return jnp.dot(_mx(a), _mx(b), preferred_element_type=F32)


def _dot_nt(a, b):
    return lax.dot_general(_mx(a), _mx(b), (((1,), (1,)), ((), ())), preferred_element_type=F32)


def _rms(x, g):
    return x * lax.rsqrt(jnp.mean(x * x, axis=-1, keepdims=True) + EPS) * g


def _log_sigmoid(x):
    return jnp.minimum(x, 0.0) - jnp.log1p(jnp.exp(-jnp.abs(x)))


def _split3(x):
    hi = x.astype(MXU_DTYPE)
    r1 = x - hi.astype(F32)
    mid = r1.astype(MXU_DTYPE)
    lo = (r1 - mid.astype(F32)).astype(MXU_DTYPE)
    return hi, mid, lo


def _full_spec(shape):
    n = len(shape)
    return pl.BlockSpec(shape, lambda *_: (0,) * n)


def _memkv_kernel(m_ref, g_ref, wk_ref, wv_ref, k_ref, v_ref):
    hn = _mx(_rms(m_ref[...], g_ref[0]))
    k_ref[0] = jnp.dot(hn, wk_ref[0], preferred_element_type=F32)
    v_ref[0] = jnp.dot(hn, wv_ref[0], preferred_element_type=F32)


def _memkv(mem2d, g, wk, wv):
    M, D = mem2d.shape
    depth = wk.shape[0]
    tm = min(M, 512)
    wspec = pl.BlockSpec((1, D, D), lambda l, i: (l, 0, 0))
    ospec = pl.BlockSpec((1, tm, D), lambda l, i: (l, i, 0))
    return pl.pallas_call(
        _memkv_kernel,
        grid=(depth, M // tm),
        in_specs=[pl.BlockSpec((tm, D), lambda l, i: (i, 0)), pl.BlockSpec((1, 1, D), lambda l, i: (l, 0, 0)), wspec, wspec],
        out_specs=[ospec, ospec],
        out_shape=[jax.ShapeDtypeStruct((depth, M, D), F32)] * 2,
        compiler_params=_cparams("parallel", "parallel"),
        name="memkv",
    )(mem2d, g, wk, wv)


def _transpose_exact(eye, rows):
    return sum(lax.dot_general(eye, p, (((1,), (1,)), ((), ())), preferred_element_type=F32) for p in _split3(rows))


def _mix_kernel(x_ref, gmix_ref, win_ref, wgt_ref, gbcol_ref, gmvg_ref, ws_ref, bst_ref,
                cw_ref, cb_ref, wqk_ref, wv_ref, outg_ref, skip_ref, wout_ref,
                xo_ref, c_ref, n_ref, m_ref, conv_ref, xe_s, ygm_s, yml_s, ca_s, m_s, *, tile, width):
    W = width
    L = min(tile, ML_CHUNK)
    D = HEAD_DIM
    R = SUBLANES

    @pl.when(pl.program_id(1) == 0)
    def _init():
        xe_s[0:SUBLANES, :] = jnp.zeros((SUBLANES, W), F32)
        ca_s[...] = jnp.zeros_like(ca_s)
        m_s[...] = jnp.zeros_like(m_s)

    x = x_ref[0]
    hn = _mx(_rms(x, gmix_ref[...]))
    g_row = _dot_nt(wgt_ref[...], hn) + gbcol_ref[...]
    ig_row = g_row[:R]
    lf_row = _log_sigmoid(g_row[R:])

    row_i = lax.broadcasted_iota(jnp.int32, (L, L), 0)
    col_i = lax.broadcasted_iota(jnp.int32, (L, L), 1)
    causal = col_i <= row_i
    triu = _mx(jnp.where(row_i <= col_i, 1.0, 0.0))
    eye = _mx(jnp.where(row_i == col_i, 1.0, 0.0))
    n_ch = tile // L
    chunks = [slice(c * L, (c + 1) * L) for c in range(n_ch)]
    b_all = jnp.concatenate([sum(jnp.dot(p, triu, preferred_element_type=F32) for p in _split3(lf_row[:, rs]))
                             for rs in chunks], axis=1)
    g_all = ig_row - b_all
    pos = lax.broadcasted_iota(jnp.int32, (R, tile), 1) % L

    NB = 2 * D
    n_blk = 4 * W // NB
    proj_blk = []
    run_max = g_all
    step = 1
    for i in range(n_blk):
        proj_blk.append(jnp.dot(hn, win_ref[:, i * NB:(i + 1) * NB], preferred_element_type=F32))
        if step < L:
            run_max = jnp.maximum(run_max, jnp.where(pos >= step, pltpu.roll(run_max, step, 1), NEG))
            step *= 2
    while step < L:
        run_max = jnp.maximum(run_max, jnp.where(pos >= step, pltpu.roll(run_max, step, 1), NEG))
        step *= 2
    proj = jnp.concatenate(proj_blk, axis=1)

    g_r, cols, decay_all = [], [], []
    m_prev = m_s[:, 0:1]
    for rs in chunks:
        b_r = b_all[:, rs]
        g = g_all[:, rs]
        M_r = jnp.maximum(run_max[:, rs], m_prev)
        M_last = M_r[:, L - 1:L]
        g_r.append(g)
        cols.append(_transpose_exact(eye, jnp.concatenate(
            [M_r, jnp.exp(m_prev - M_r), jnp.exp(-(b_r + M_r)), jnp.exp(g - M_last)], axis=0)))
        decay_all.append(jnp.exp(m_prev - M_last))
        m_prev = b_r[:, L - 1:L] + M_last
    m_s[...] = jnp.broadcast_to(m_prev, (R, LANES))

    xm = proj[:, :W]
    xe_s[SUBLANES:SUBLANES + tile, :] = xm
    conv = cb_ref[...] + cw_ref[ML_CONV - 1:ML_CONV, :] * xm
    for j in range(1, ML_CONV):
        conv = conv + cw_ref[ML_CONV - 1 - j:ML_CONV - j, :] * xe_s[SUBLANES - j:SUBLANES - j + tile, :]
    ca = conv * jax.nn.sigmoid(conv)
    conv_ref[0] = xe_s[SUBLANES + tile - (ML_CONV - 1):SUBLANES + tile, :]
    xe_s[0:SUBLANES, :] = xe_s[tile:tile + SUBLANES, :]
    ones_blk = jnp.ones((tile, D), F32)
    k_scale = D ** -0.5
    q, k, va = [], [], []
    for h in range(HEADS):
        cs = slice(h * D, (h + 1) * D)
        qk = jnp.dot(_mx(ca[:, cs]), wqk_ref[h], preferred_element_type=F32)
        q.append(_mx(qk[:, :D]))
        k.append(qk[:, D:] * k_scale)
        v = jnp.dot(_mx(xm[:, cs]), wv_ref[h], preferred_element_type=F32)
        va.append(_mx(jnp.concatenate([v, ones_blk], axis=1)))

    z = jax.nn.gelu(proj[:, W:3 * W])
    u = z[:, :W]
    v_gm = _rms(z[:, W:], gmvg_ref[...])
    G = GM_CHUNK
    g_causal = lax.broadcasted_iota(jnp.int32, (G, G), 1) <= lax.broadcasted_iota(jnp.int32, (G, G), 0)
    n_gm = tile // G
    for h in range(HEADS):
        cs = slice(h * D, (h + 1) * D)
        w_h = _mx(jnp.where(g_causal, ws_ref[h], 0.0))
        b_h = bst_ref[:, h:h + 1]
        for c0 in range(0, n_gm, 2):
            cc = list(range(c0, min(c0 + 2, n_gm)))
            vv = jnp.concatenate([_mx(v_gm[c * G:(c + 1) * G, cs]) for c in cc], axis=1)
            mixed = jnp.dot(w_h, vv, preferred_element_type=F32) + b_h
            for i, c in enumerate(cc):
                rs = slice(c * G, (c + 1) * G)
                ygm_s[rs, cs] = u[rs, cs] * mixed[:, i * D:(i + 1) * D]
    acc = x + jnp.dot(_mx(ygm_s[...]), wout_ref[0:W, :], preferred_element_type=F32)

    pairs = [(c, h) for c in range(n_ch) for h in range(HEADS)]

    def scores(c, h):
        return _dot_nt(q[h][chunks[c]], k[h][chunks[c]])

    Ca = [ca_s[h] for h in range(HEADS)]
    s_next = scores(*pairs[0])
    for i, (c, h) in enumerate(pairs):
        rs = chunks[c]
        cs = slice(h * D, (h + 1) * D)
        s_raw = s_next
        if i + 1 < len(pairs):
            s_next = scores(*pairs[i + 1])
        decay_mat = jnp.exp(jnp.where(causal, g_r[c][h:h + 1, :] - cols[c][:, h:h + 1], NEG))
        sv = jnp.dot(_mx(s_raw * decay_mat), va[h][rs], preferred_element_type=F32)
        qc = jnp.dot(q[h][rs], _mx(Ca[h]), preferred_element_type=F32)
        w_inter = cols[c][:, R + h:R + h + 1]
        floor = cols[c][:, 2 * R + h:2 * R + h + 1]
        w_s = cols[c][:, 3 * R + h:3 * R + h + 1]
        Ca[h] = decay_all[c][h:h + 1, :] * Ca[h] + jnp.dot(_mx(jnp.transpose(k[h][rs] * w_s)), va[h][rs],
                                                           preferred_element_type=F32)
        num = sv[:, :D] + w_inter * qc[:, :D]
        den = sv[:, D:] + w_inter * qc[:, D:]
        hc = num / jnp.maximum(jnp.abs(den), floor)
        hcn = _rms(hc, outg_ref[:, cs])
        yml_s[rs, cs] = jax.nn.sigmoid(proj[rs, 3 * W + h * D:3 * W + (h + 1) * D]) * (hcn + skip_ref[:, cs] * ca[rs, cs])
    for h in range(HEADS):
        ca_s[h] = Ca[h]

    xo_ref[0] = acc + jnp.dot(_mx(yml_s[...]), wout_ref[W:2 * W, :], preferred_element_type=F32)

    @pl.when(pl.program_id(1) == pl.num_programs(1) - 1)
    def _emit_state():
        for h in range(HEADS):
            c_ref[0, h] = ca_s[h, :, :D]
            n_ref[0, h:h + 1, :] = jnp.transpose(ca_s[h, :, D:])[0:1, :]
        m_ref[0] = m_s[0:HEADS, :]


def _mix(x, p):
    B, S, Dm = x.shape
    W = HEADS * HEAD_DIM
    tile = min(S, MIX_TILE)
    assert S % tile == 0 and tile % GM_CHUNK == 0 and tile % min(tile, ML_CHUNK) == 0
    weights = (p["g_mix"], p["w_in_main"], p["w_gt"], p["gb_col"], p["gm_v_g"], p["gm_ws"],
               p["gm_bst"], p["conv_w"], p["conv_b"], p["wqk"], p["wv"], p["out_g"], p["skip"], p["w_out"])
    return pl.pallas_call(
        functools.partial(_mix_kernel, tile=tile, width=W),
        grid=(B, S // tile),
        in_specs=[pl.BlockSpec((1, tile, Dm), lambda b, s: (b, s, 0))] + [_full_spec(w.shape) for w in weights],
        out_specs=[pl.BlockSpec((1, tile, Dm), lambda b, s: (b, s, 0)),
                   pl.BlockSpec((1, HEADS, HEAD_DIM, HEAD_DIM), lambda b, s: (b, 0, 0, 0)),
                   pl.BlockSpec((1, HEADS, HEAD_DIM), lambda b, s: (b, 0, 0)),
                   pl.BlockSpec((1, HEADS, LANES), lambda b, s: (b, 0, 0)),
                   pl.BlockSpec((1, ML_CONV - 1, W), lambda b, s: (b, 0, 0))],
        out_shape=[jax.ShapeDtypeStruct((B, S, Dm), F32),
                   jax.ShapeDtypeStruct((B, HEADS, HEAD_DIM, HEAD_DIM), F32),
                   jax.ShapeDtypeStruct((B, HEADS, HEAD_DIM), F32),
                   jax.ShapeDtypeStruct((B, HEADS, LANES), F32),
                   jax.ShapeDtypeStruct((B, ML_CONV - 1, W), F32)],
        scratch_shapes=[pltpu.VMEM((tile + SUBLANES, W), F32), pltpu.VMEM((tile, W), F32), pltpu.VMEM((tile, W), F32),
                        pltpu.VMEM((HEADS, HEAD_DIM, 2 * HEAD_DIM), F32), pltpu.VMEM((SUBLANES, LANES), F32)],
        compiler_params=_cparams("parallel", "arbitrary"),
        name="mix",
    )(x, *weights)


def _xattn_kernel(x_ref, g_ref, wq_ref, k_ref, v_ref, wo_ref, o_ref, *, heads):
    x = x_ref[0]
    hq = jnp.dot(_mx(_rms(x, g_ref[...])), wq_ref[...], preferred_element_type=F32)
    dh = x.shape[1] // heads
    scale = dh ** -0.5
    cols = [slice(h * dh, (h + 1) * dh) for h in range(heads)]

    def scores(h):
        return _dot_nt(hq[:, cols[h]], k_ref[0, :, cols[h]]) * scale

    acc = x
    pairs = [list(range(h0, min(h0 + 2, heads))) for h0 in range(0, heads, 2)]
    sc_next = [scores(h) for h in pairs[0]]
    for i, hs in enumerate(pairs):
        sc = sc_next
        if i + 1 < len(pairs):
            sc_next = [scores(h) for h in pairs[i + 1]]
        es = [jnp.exp(s_ - jnp.max(s_, axis=-1, keepdims=True)) for s_ in sc]
        probs = [e / jnp.sum(e, axis=-1, keepdims=True) for e in es]
        att = jnp.concatenate([_dot(p_, v_ref[0, :, cols[h]]) for p_, h in zip(probs, hs)], axis=-1)
        acc = acc + jnp.dot(_mx(att), wo_ref[hs[0] * dh:(hs[-1] + 1) * dh, :], preferred_element_type=F32)
    o_ref[0] = acc


def _xattn(x, g, wq, mk, mv, wo, layer):
    B, S, Dm = x.shape
    Mt = mk.shape[1]
    kv = pl.BlockSpec((1, Mt, Dm), lambda b, s: (layer * B + b, 0, 0))
    tile = min(S, XA_TILE)
    assert S % tile == 0
    return pl.pallas_call(
        functools.partial(_xattn_kernel, heads=HEADS),
        grid=(B, S // tile),
        in_specs=[pl.BlockSpec((1, tile, Dm), lambda b, s: (b, s, 0)), _full_spec(g.shape), _full_spec(wq.shape),
                  kv, kv, _full_spec(wo.shape)],
        out_specs=pl.BlockSpec((1, tile, Dm), lambda b, s: (b, s, 0)),
        out_shape=jax.ShapeDtypeStruct((B, S, Dm), F32),
        compiler_params=_cparams("parallel", "parallel"),
        name="xattn",
    )(x, g, wq, mk, mv, wo)


def _ffn_kernel(x_ref, g_ref, wu_ref, wd_ref, gf_ref, o_ref, *, final):
    x = x_ref[...]
    hn = _mx(_rms(x, g_ref[...]))
    acc = x
    dff = wu_ref.shape[1]
    fc = min(dff, FFN_CHUNK)
    for c in range(dff // fc):
        up = jnp.dot(hn, wu_ref[:, c * fc:(c + 1) * fc], preferred_element_type=F32)
        act = jnp.square(jnp.maximum(up, 0.0))
        acc = acc + jnp.dot(_mx(act), wd_ref[c * fc:(c + 1) * fc, :], preferred_element_type=F32)
    o_ref[...] = _rms(acc, gf_ref[...]) if final else acc


def _ffn(x2d, g, wu, wd, gf, final):
    M, Dm = x2d.shape
    tile = min(M, FFN_TILE)
    assert M % tile == 0
    return pl.pallas_call(
        functools.partial(_ffn_kernel, final=final),
        grid=(M // tile,),
        in_specs=[pl.BlockSpec((tile, Dm), lambda i: (i, 0)), _full_spec(g.shape), _full_spec(wu.shape),
                  _full_spec(wd.shape), _full_spec(gf.shape)],
        out_specs=pl.BlockSpec((tile, Dm), lambda i: (i, 0)),
        out_shape=jax.ShapeDtypeStruct((M, Dm), F32),
        compiler_params=_cparams("parallel"),
        name="ffn",
    )(x2d, g, wu, wd, gf)


def _s_front_kernel(x_ref, gmix_ref, win_ref, wg_ref, gbrow_ref, gmvg_ref, ws0_ref, bs0_ref, cw_ref, cb_ref,
                    cst_ref, wq_ref, wk_ref, wv_ref, wqt_ref, wkt_ref, n0_ref, m0_ref, skip_ref,
                    ygm_ref, gmv_ref, cnew_ref, qt_ref, kt_ref, vs_ref, a_ref, bc_ref, nn_ref, mt_ref, wi_ref,
                    og_ref, sc_ref, *, width):
    W = width
    D = HEAD_DIM
    x = x_ref[...]
    hn = _mx(_rms(x, gmix_ref[...]))
    proj = jnp.dot(hn, win_ref[...], preferred_element_type=F32)
    g_col = jnp.dot(hn, wg_ref[...], preferred_element_type=F32) + gbrow_ref[...]
    ig = g_col[:, :LANES]
    lf = _log_sigmoid(g_col[:, LANES:])

    z = jax.nn.gelu(proj[:, W:3 * W])
    v_gm = _rms(z[:, W:], gmvg_ref[...])
    gmv_ref[...] = v_gm
    ygm_ref[...] = z[:, :W] * (ws0_ref[...] * v_gm + bs0_ref[...])

    xm = proj[:, :W]
    conv = cb_ref[...] + cw_ref[ML_CONV - 1:ML_CONV, :] * xm
    for j in range(ML_CONV - 1):
        conv = conv + cw_ref[j:j + 1, :] * cst_ref[:, j * W:(j + 1) * W]
    ca = conv * jax.nn.sigmoid(conv)
    cnew_ref[:, :(ML_CONV - 2) * W] = cst_ref[:, W:]
    cnew_ref[:, (ML_CONV - 2) * W:] = xm
    og_ref[...] = jax.nn.sigmoid(proj[:, 3 * W:])
    sc_ref[...] = skip_ref[...] * ca

    a = lf + m0_ref[...]
    mt = jnp.maximum(a, ig)
    w_inter = jnp.exp(a - mt)
    w_in_gate = jnp.exp(ig - mt)
    floor = jnp.exp(-mt)
    mt_ref[...] = mt
    wi_ref[...] = w_inter
    k_scale = D ** -0.5
    for h in range(HEADS):
        cs = slice(h * D, (h + 1) * D)
        ca_h = _mx(ca[:, cs])
        q = jnp.dot(ca_h, wq_ref[h], preferred_element_type=F32)
        k = jnp.dot(ca_h, wk_ref[h], preferred_element_type=F32) * k_scale
        v = jnp.dot(_mx(xm[:, cs]), wv_ref[h], preferred_element_type=F32)
        qt_ref[h] = _dot_nt(wqt_ref[h], ca_h)
        kt_ref[h] = _dot_nt(wkt_ref[h], ca_h) * k_scale
        wi_h = w_inter[:, h:h + 1]
        wg_h = w_in_gate[:, h:h + 1]
        n0 = n0_ref[:, cs]
        s = jnp.sum(q * k, axis=1, keepdims=True) * wg_h
        den = s + wi_h * jnp.sum(q * n0, axis=1, keepdims=True)
        dnm = jnp.maximum(jnp.abs(den), floor[:, h:h + 1])
        a_ref[:, cs] = s * v / dnm
        bc_ref[:, cs] = jnp.broadcast_to(wi_h / dnm, (x.shape[0], D))
        vs_ref[:, cs] = wg_h * v
        nn_ref[:, cs] = wi_h * n0 + wg_h * k


def _s_front(x, cst, n0, m0p, p):
    N, Dm = x.shape
    W = HEADS * HEAD_DIM
    ins = (x, p["g_mix"], p["w_in_main"], p["w_g"], p["gb_row"], p["gm_v_g"], p["gm_ws0"], p["gm_bs0"],
           p["conv_w"], p["conv_b"], cst, p["wq"], p["wk"], p["wv"], p["wqt"], p["wkt"], n0, m0p, p["skip"])
    row = jax.ShapeDtypeStruct((N, W), F32)
    colw = jax.ShapeDtypeStruct((N, LANES), F32)
    tr = jax.ShapeDtypeStruct((HEADS, HEAD_DIM, N), F32)
    outs = [row, row, jax.ShapeDtypeStruct((N, (ML_CONV - 1) * W), F32), tr, tr, row, row, row, row, colw, colw, row, row]
    return pl.pallas_call(
        functools.partial(_s_front_kernel, width=W),
        grid=(1,),
        in_specs=[_full_spec(a.shape) for a in ins],
        out_specs=[_full_spec(o.shape) for o in outs],
        out_shape=outs,
        compiler_params=_cparams("arbitrary"),
        name="s_front",
    )(*ins)


def _s_state_kernel(dec_ref, c_ref, qt_ref, kt_ref, vs_ref, *rest, block, nseq):
    cn_ref, qc_ref = rest[-2:]
    D = HEAD_DIM

    @pl.when(pl.program_id(0) == 0)
    def _update():
        base = pl.program_id(1) * block
        shift = (nseq - base) % nseq
        for h in range(HEADS):
            qt = pltpu.roll(qt_ref[h], shift, 1)
            kt = pltpu.roll(kt_ref[h], shift, 1)
            for j in range(block):
                C = c_ref[0, j, h]
                qc_ref[j:j + 1, h * D:(h + 1) * D] = jnp.sum(C * qt[:, j:j + 1], axis=0, keepdims=True)
                decay = dec_ref[(base + j) * HEADS + h]
                cn_ref[0, j, h] = decay * C + kt[:, j:j + 1] * vs_ref[j:j + 1, h * D:(h + 1) * D]

    @pl.when(pl.program_id(0) > 0)
    def _fill():
        cn_ref[...] = jnp.zeros_like(cn_ref)


def _s_state(dec, C_all, qt, kt, vs, layer, C_new_all):
    depth, N = C_all.shape[:2]
    W = HEADS * HEAD_DIM
    bb = min(N, STATE_BLOCK)
    assert N % bb == 0
    nblk = N // bb
    passes = depth if C_new_all is None else 1
    blk = lambda p, i: jnp.where(p == 0, i, nblk - 1)
    cin = pl.BlockSpec((1, bb, HEADS, HEAD_DIM, HEAD_DIM), lambda p, i: (layer, blk(p, i), 0, 0, 0))
    cout = pl.BlockSpec((1, bb, HEADS, HEAD_DIM, HEAD_DIM), lambda p, i: ((layer + p) % depth, i, 0, 0, 0))
    rblk = pl.BlockSpec((bb, W), lambda p, i: (blk(p, i), 0))
    ins = [dec, C_all, qt, kt, vs]
    in_specs = [pl.BlockSpec(memory_space=pltpu.SMEM), cin, _full_spec(qt.shape), _full_spec(kt.shape), rblk]
    aliases = {}
    if C_new_all is not None:
        ins.append(C_new_all)
        in_specs.append(pl.BlockSpec(memory_space=pl.ANY))
        aliases = {len(ins) - 1: 0}
    return pl.pallas_call(
        functools.partial(_s_state_kernel, block=bb, nseq=N),
        grid=(passes, nblk),
        in_specs=in_specs,
        out_specs=[cout, rblk],
        out_shape=[jax.ShapeDtypeStruct(C_all.shape, F32), jax.ShapeDtypeStruct((N, W), F32)],
        input_output_aliases=aliases,
        compiler_params=_cparams("arbitrary", "arbitrary"),
        name="s_state",
    )(*ins)


def _s_back_kernel(x_ref, ygm_ref, a_ref, bc_ref, qc_ref, og_ref, sc_ref, outg_ref, wout_ref, gxa_ref, wcq_ref,
                   x1_ref, hq_ref, *, width):
    W = width
    D = HEAD_DIM
    hc = a_ref[...] + bc_ref[...] * qc_ref[...]
    parts = [ygm_ref[...]]
    for h in range(HEADS):
        cs = slice(h * D, (h + 1) * D)
        parts.append(og_ref[:, cs] * (_rms(hc[:, cs], outg_ref[:, cs]) + sc_ref[:, cs]))
    y = jnp.concatenate(parts, axis=-1)
    x1 = x_ref[...] + jnp.dot(_mx(y), wout_ref[...], preferred_element_type=F32)
    x1_ref[...] = x1
    hq_ref[...] = jnp.dot(_mx(_rms(x1, gxa_ref[...])), wcq_ref[...], preferred_element_type=F32)


def _s_back(x, ygm, a, bc, qc, og, sc, p):
    N, Dm = x.shape
    ins = (x, ygm, a, bc, qc, og, sc, p["out_g"], p["w_out"], p["g_xa"], p["w_cq"])
    out = jax.ShapeDtypeStruct((N, Dm), F32)
    return pl.pallas_call(
        functools.partial(_s_back_kernel, width=HEADS * HEAD_DIM),
        grid=(1,),
        in_specs=[_full_spec(a_.shape) for a_ in ins],
        out_specs=[_full_spec(out.shape)] * 2,
        out_shape=[out, out],
        compiler_params=_cparams("arbitrary"),
        name="s_back",
    )(*ins)


def _s_attn_kernel(hq_ref, k_ref, v_ref, o_ref, *, block):
    scale = hq_ref.shape[-1] ** -0.5
    for j in range(block):
        q = hq_ref[0, j]
        sc = jnp.sum(k_ref[0, j] * q[None], axis=-1, keepdims=True) * scale
        e = jnp.exp(sc - jnp.max(sc, axis=0, keepdims=True))
        prob = e / jnp.sum(e, axis=0, keepdims=True)
        o_ref[0, j] = jnp.sum(v_ref[0, j] * prob, axis=0)


def _s_attn(hq, K_all, V_all, layer):
    N, Dm = hq.shape
    _, _, Mt, H, dh = K_all.shape
    bb = min(N, STATE_BLOCK)
    assert N % bb == 0
    kv = pl.BlockSpec((1, bb, Mt, H, dh), lambda i: (layer, i, 0, 0, 0))
    qo = pl.BlockSpec((1, bb, H, dh), lambda i: (i, 0, 0, 0))
    out = pl.pallas_call(
        functools.partial(_s_attn_kernel, block=bb),
        grid=(N // bb,),
        in_specs=[qo, kv, kv],
        out_specs=qo,
        out_shape=jax.ShapeDtypeStruct((N // bb, bb, H, dh), F32),
        compiler_params=_cparams("parallel"),
        name="s_attn",
    )(hq.reshape(N // bb, bb, H, dh), K_all, V_all)
    return out.reshape(N, Dm)


def _s_post_kernel(x1_ref, att_ref, wco_ref, gffn_ref, wu_ref, wd_ref, gf_ref, o_ref, hn_s, *, final):
    i = pl.program_id(0)

    @pl.when(i == 0)
    def _first():
        x2 = x1_ref[...] + jnp.dot(_mx(att_ref[...]), wco_ref[...], preferred_element_type=F32)
        o_ref[...] = x2
        hn_s[...] = _rms(x2, gffn_ref[...])

    up = jnp.dot(_mx(hn_s[...]), wu_ref[...], preferred_element_type=F32)
    act = jnp.square(jnp.maximum(up, 0.0))
    o_ref[...] += jnp.dot(_mx(act), wd_ref[...], preferred_element_type=F32)

    if final:
        @pl.when(i == pl.num_programs(0) - 1)
        def _last():
            o_ref[...] = _rms(o_ref[...], gf_ref[...])


def _s_post(x1, att, p, gf, final):
    N, Dm = x1.shape
    dff = p["w_up"].shape[1]
    fc = min(dff, FFN_CHUNK)
    return pl.pallas_call(
        functools.partial(_s_post_kernel, final=final),
        grid=(dff // fc,),
        in_specs=[_full_spec((N, Dm)), _full_spec((N, Dm)), _full_spec(p["w_co"].shape), _full_spec((1, Dm)),
                  pl.BlockSpec((Dm, fc), lambda i: (0, i)), pl.BlockSpec((fc, Dm), lambda i: (i, 0)),
                  _full_spec((1, Dm))],
        out_specs=_full_spec((N, Dm)),
        out_shape=jax.ShapeDtypeStruct((N, Dm), F32),
        scratch_shapes=[pltpu.VMEM((N, Dm), F32)],
        compiler_params=_cparams("arbitrary"),
        name="s_post",
    )(x1, att, p["w_co"], p["g_ffn"], p["w_up"], p["w_down"], gf)


def _layer_params(l, norm_mix_g, w_in, gm_v_norm_g, gm_ws, gm_bs, ml_conv_w, ml_conv_b, ml_wq, ml_wk, ml_wv,
                  ml_b_i, ml_b_f, ml_out_norm_g, ml_skip, w_out, norm_mem_g, w_ck, w_cv, norm_xa_g, w_cq, w_co,
                  norm_ffn_g, w_up, w_down):
    W = HEADS * HEAD_DIM
    Dm = w_in.shape[1]
    row = lambda a: a.reshape(1, -1).astype(F32)
    wi = w_in[l]
    w_gate = wi[:, 4 * W:]
    w_g = jnp.zeros((Dm, 2 * LANES), F32).at[:, :HEADS].set(w_gate[:, :HEADS]).at[:, LANES:LANES + HEADS].set(w_gate[:, HEADS:])
    w_gt = jnp.zeros((2 * SUBLANES, Dm), F32).at[:HEADS].set(w_gate[:, :HEADS].T).at[SUBLANES:SUBLANES + HEADS].set(w_gate[:, HEADS:].T)
    gb_row = jnp.zeros((1, 2 * LANES), F32).at[0, :HEADS].set(ml_b_i[l]).at[0, LANES:LANES + HEADS].set(ml_b_f[l])
    gb_col = jnp.zeros((2 * SUBLANES, 1), F32).at[:HEADS, 0].set(ml_b_i[l]).at[SUBLANES:SUBLANES + HEADS, 0].set(ml_b_f[l])
    return {
        "g_mix": row(norm_mix_g[l]), "w_in_main": _mx(jnp.concatenate([wi[:, 2 * W:3 * W], wi[:, :2 * W], wi[:, 3 * W:4 * W]], axis=1)), "w_g": _mx(w_g), "w_gt": _mx(w_gt),
        "gb_row": gb_row, "gb_col": gb_col, "gm_v_g": row(gm_v_norm_g[l]), "gm_ws": gm_ws[l], "gm_bst": gm_bs[l].T,
        "gm_ws0": row(jnp.repeat(gm_ws[l][:, 0, 0], HEAD_DIM)), "gm_bs0": row(jnp.repeat(gm_bs[l][:, 0], HEAD_DIM)),
        "conv_w": ml_conv_w[l], "conv_b": row(ml_conv_b[l]),
        "wq": _mx(ml_wq[l]), "wk": _mx(ml_wk[l]), "wv": _mx(ml_wv[l]),
        "wqk": _mx(jnp.concatenate([ml_wq[l], ml_wk[l]], axis=-1)),
        "wqt": _mx(jnp.swapaxes(ml_wq[l], 1, 2)), "wkt": _mx(jnp.swapaxes(ml_wk[l], 1, 2)),
        "out_g": row(ml_out_norm_g[l]), "skip": row(ml_skip[l]), "w_out": _mx(w_out[l]),
        "g_xa": row(norm_xa_g[l]), "w_cq": _mx(w_cq[l]), "w_co": _mx(w_co[l]),
        "g_ffn": row(norm_ffn_g[l]), "w_up": _mx(w_up[l]), "w_down": _mx(w_down[l]),
    }


def kernel(x_prompt, x_sample, mem_prompt, cache_mem_k, cache_mem_v, state_C, state_n, state_m, state_conv, norm_mix_g, w_in, gm_v_norm_g, gm_ws, gm_bs, ml_conv_w, ml_conv_b, ml_wq, ml_wk, ml_wv, ml_b_i, ml_b_f, ml_out_norm_g, ml_skip, w_out, norm_mem_g, w_ck, w_cv, norm_xa_g, w_cq, w_co, norm_ffn_g, w_up, w_down, norm_f_g):
    depth = w_in.shape[0]
    B, S, Dm = x_prompt.shape
    N = x_sample.shape[0]
    assert x_sample.shape[1] == 1
    Mt = mem_prompt.shape[1]
    W = HEADS * HEAD_DIM
    xa_dh = Dm // HEADS
    gf = norm_f_g.reshape(1, Dm).astype(F32)

    xp = x_prompt
    xs = x_sample.reshape(N, Dm)
    mem2d = mem_prompt.reshape(B * Mt, Dm)
    mk_all, mv_all = _memkv(mem2d, norm_mem_g.reshape(depth, 1, Dm), _mx(w_ck), _mx(w_cv))
    mk_rows = mk_all.reshape(depth * B, Mt, Dm)
    mv_rows = mv_all.reshape(depth * B, Mt, Dm)
    C_p, n_p, m_p, cv_p = [], [], [], []
    n_s, m_s, cv_s, gv_s = [], [], [], []
    C_s = None
    for l in range(depth):
        p = _layer_params(l, norm_mix_g, w_in, gm_v_norm_g, gm_ws, gm_bs, ml_conv_w, ml_conv_b, ml_wq, ml_wk, ml_wv,
                          ml_b_i, ml_b_f, ml_out_norm_g, ml_skip, w_out, norm_mem_g, w_ck, w_cv, norm_xa_g, w_cq,
                          w_co, norm_ffn_g, w_up, w_down)
        final = l == depth - 1

        xp, Cp, np_, mp, cvp = _mix(xp, p)
        xp = _xattn(xp, p["g_xa"], p["w_cq"], mk_rows, mv_rows, p["w_co"], l)
        xp = _ffn(xp.reshape(B * S, Dm), p["g_ffn"], p["w_up"], p["w_down"], gf, final).reshape(B, S, Dm)
        C_p.append(Cp)
        n_p.append(np_)
        m_p.append(mp[:, :, 0])
        cv_p.append(cvp)

        m0p = jnp.zeros((N, LANES), F32).at[:, :HEADS].set(state_m[l])
        (ygm, gmv, cnew, qt, kt, vs, a, bc, nn, mt, wi, og, sc) = _s_front(
            xs, state_conv[l].reshape(N, (ML_CONV - 1) * W), state_n[l].reshape(N, W), m0p, p)
        C_s, qc = _s_state(wi[:, :HEADS].reshape(N * HEADS), state_C, qt, kt, vs, l, C_s)
        x1, hq = _s_back(xs, ygm, a, bc, qc, og, sc, p)
        att = _s_attn(hq, cache_mem_k, cache_mem_v, l)
        xs = _s_post(x1, att, p, gf, final)
        n_s.append(nn.reshape(N, HEADS, HEAD_DIM))
        m_s.append(mt[:, :HEADS])
        cv_s.append(cnew.reshape(N, ML_CONV - 1, W))
        gv_s.append(gmv.reshape(N, 1, W))

    st = lambda a: jnp.stack(a, axis=0)
    return (xp, xs.reshape(N, 1, Dm), mk_all.reshape(depth, B, Mt, HEADS, xa_dh), mv_all.reshape(depth, B, Mt, HEADS, xa_dh),
            st(C_p), st(n_p), st(m_p), st(cv_p), C_s, st(n_s), st(m_s), st(cv_s), st(gv_s))
```

```python
import functools

import jax
import jax.numpy as jnp
from jax import lax
from jax.experimental import pallas as pl
from jax.experimental.pallas import tpu as pltpu

F32 = jnp.float32
MXU_DTYPE = jnp.bfloat16

EPS = 1e-6
NEG = -1e30
LANES = 128
SUBLANES = 8
HEADS = 4
HEAD_DIM = 128
GM_CHUNK = 128
ML_CHUNK = 256
ML_CONV = 4
MIX_TILE = 512
XA_TILE = 512
FFN_TILE = 512
FFN_CHUNK = 1024
STATE_BLOCK = 8
VMEM_LIMIT = 56 * 1024 * 1024


def _cparams(*sem):
    return pltpu.CompilerParams(dimension_semantics=sem, vmem_limit_bytes=VMEM_LIMIT)


def _mx(a):
    return a.astype(MXU_DTYPE)


def _dot(a, b):
    return jnp.dot(_mx(a), _mx(b), preferred_element_type=F32)


def _dot_nt(a, b):
    return lax.dot_general(_mx(a), _mx(b), (((1,), (1,)), ((), ())), preferred_element_type=F32)


def _rms(x, g):
    return x * lax.rsqrt(jnp.mean(x * x, axis=-1, keepdims=True) + EPS) * g


def _log_sigmoid(x):
    return jnp.minimum(x, 0.0) - jnp.log1p(jnp.exp(-jnp.abs(x)))


def _split3(x):
    hi = x.astype(MXU_DTYPE)
    r1 = x - hi.astype(F32)
    mid = r1.astype(MXU_DTYPE)
    lo = (r1 - mid.astype(F32)).astype(MXU_DTYPE)
    return hi, mid, lo


def _full_spec(shape):
    n = len(shape)
    return pl.BlockSpec(shape, lambda *_: (0,) * n)


class _LayerOf:
    def __init__(self, stacked, layer):
        self.stacked, self.layer, self.shape = stacked, layer, tuple(stacked.shape[1:])


def _spec(a):
    if isinstance(a, _LayerOf):
        n, layer = len(a.shape), a.layer
        return pl.BlockSpec((None,) + a.shape, lambda *_: (layer,) + (0,) * n)
    return _full_spec(a.shape)


def _arr(a):
    return a.stacked if isinstance(a, _LayerOf) else a


def _memkv_kernel(m_ref, g_ref, wk_ref, wv_ref, k_ref, v_ref, k4_ref, v4_ref):
    hn = _mx(_rms(m_ref[...], g_ref[0]))
    heads, dh = k4_ref.shape[2:]
    for w_ref, o_ref, o4_ref in ((wk_ref, k_ref, k4_ref), (wv_ref, v_ref, v4_ref)):
        y = jnp.dot(hn, w_ref[0], preferred_element_type=F32)
        o_ref[0] = y
        for h in range(heads):
            o4_ref[0, :, h, :] = y[:, h * dh:(h + 1) * dh]


def _memkv(mem2d, g, wk, wv, heads):
    M, D = mem2d.shape
    depth = wk.shape[0]
    tm = min(M, 512)
    wspec = pl.BlockSpec((1, D, D), lambda l, i: (l, 0, 0))
    ospec = pl.BlockSpec((1, tm, D), lambda l, i: (l, i, 0))
    o4spec = pl.BlockSpec((1, tm, heads, D // heads), lambda l, i: (l, i, 0, 0))
    rows = jax.ShapeDtypeStruct((depth, M, D), F32)
    split = jax.ShapeDtypeStruct((depth, M, heads, D // heads), F32)
    return pl.pallas_call(
        _memkv_kernel,
        grid=(depth, M // tm),
        in_specs=[pl.BlockSpec((tm, D), lambda l, i: (i, 0)), pl.BlockSpec((1, 1, D), lambda l, i: (l, 0, 0)), wspec, wspec],
        out_specs=[ospec, ospec, o4spec, o4spec],
        out_shape=[rows, rows, split, split],
        compiler_params=_cparams("parallel", "parallel"),
        name="memkv",
    )(mem2d, g, wk, wv)


def _transpose_exact(eye, rows):
    return sum(lax.dot_general(eye, p, (((1,), (1,)), ((), ())), preferred_element_type=F32) for p in _split3(rows))


def _mix_kernel(x_ref, gmix_ref, win_ref, wgt_ref, gbcol_ref, gmvg_ref, ws_ref, bst_ref,
                cw_ref, cb_ref, wqk_ref, wv_ref, outg_ref, skip_ref, wout_ref,
                xo_ref, c_ref, n_ref, m_ref, conv_ref, xe_s, ygm_s, yml_s, ca_s, m_s, *, tile, width):
    W = width
    L = min(tile, ML_CHUNK)
    D = HEAD_DIM
    R = SUBLANES

    @pl.when(pl.program_id(1) == 0)
    def _init():
        xe_s[0:SUBLANES, :] = jnp.zeros((SUBLANES, W), F32)
        ca_s[...] = jnp.zeros_like(ca_s)
        m_s[...] = jnp.zeros_like(m_s)

    x = x_ref[0]
    hn = _mx(_rms(x, gmix_ref[...]))
    g_row = _dot_nt(wgt_ref[...], hn) + gbcol_ref[...]
    ig_row = g_row[:R]
    lf_row = _log_sigmoid(g_row[R:])

    row_i = lax.broadcasted_iota(jnp.int32, (L, L), 0)
    col_i = lax.broadcasted_iota(jnp.int32, (L, L), 1)
    causal = col_i <= row_i
    triu = _mx(jnp.where(row_i <= col_i, 1.0, 0.0))
    eye = _mx(jnp.where(row_i == col_i, 1.0, 0.0))
    n_ch = tile // L
    chunks = [slice(c * L, (c + 1) * L) for c in range(n_ch)]
    b_all = jnp.concatenate([sum(jnp.dot(p, triu, preferred_element_type=F32) for p in _split3(lf_row[:, rs]))
                             for rs in chunks], axis=1)
    g_all = ig_row - b_all
    pos = lax.broadcasted_iota(jnp.int32, (R, tile), 1) % L

    NB = 2 * D
    starts = [c for lo, hi in ((2 * W, 3 * W), (0, 2 * W), (3 * W, 4 * W)) for c in range(lo, hi, NB)]
    proj_blk = []
    run_max = g_all
    step = 1
    for c0 in starts:
        proj_blk.append(jnp.dot(hn, win_ref[:, c0:c0 + NB], preferred_element_type=F32))
        if step < L:
            run_max = jnp.maximum(run_max, jnp.where(pos >= step, pltpu.roll(run_max, step, 1), NEG))
            step *= 2
    while step < L:
        run_max = jnp.maximum(run_max, jnp.where(pos >= step, pltpu.roll(run_max, step, 1), NEG))
        step *= 2
    proj = jnp.concatenate(proj_blk, axis=1)

    g_r, cols, decay_all = [], [], []
    m_prev = m_s[:, 0:1]
    for rs in chunks:
        b_r = b_all[:, rs]
        g = g_all[:, rs]
        M_r = jnp.maximum(run_max[:, rs], m_prev)
        M_last = M_r[:, L - 1:L]
        g_r.append(g)
        cols.append(_transpose_exact(eye, jnp.concatenate(
            [M_r, jnp.exp(m_prev - M_r), jnp.exp(-(b_r + M_r)), jnp.exp(g - M_last)], axis=0)))
        decay_all.append(jnp.exp(m_prev - M_last))
        m_prev = b_r[:, L - 1:L] + M_last
    m_s[...] = jnp.broadcast_to(m_prev, (R, LANES))

    xm = proj[:, :W]
    xe_s[SUBLANES:SUBLANES + tile, :] = xm
    conv = cb_ref[...] + cw_ref[ML_CONV - 1:ML_CONV, :] * xm
    for j in range(1, ML_CONV):
        conv = conv + cw_ref[ML_CONV - 1 - j:ML_CONV - j, :] * xe_s[SUBLANES - j:SUBLANES - j + tile, :]
    ca = conv * jax.nn.sigmoid(conv)
    conv_ref[0] = xe_s[SUBLANES + tile - (ML_CONV - 1):SUBLANES + tile, :]
    xe_s[0:SUBLANES, :] = xe_s[tile:tile + SUBLANES, :]
    ones_blk = jnp.ones((tile, D), F32)
    k_scale = D ** -0.5
    q, k, va = [], [], []
    for h in range(HEADS):
        cs = slice(h * D, (h + 1) * D)
        qk = jnp.dot(_mx(ca[:, cs]), wqk_ref[h], preferred_element_type=F32)
        q.append(_mx(qk[:, :D]))
        k.append(qk[:, D:] * k_scale)
        v = jnp.dot(_mx(xm[:, cs]), wv_ref[h], preferred_element_type=F32)
        va.append(_mx(jnp.concatenate([v, ones_blk], axis=1)))

    z = jax.nn.gelu(proj[:, W:3 * W])
    u = z[:, :W]
    v_gm = _rms(z[:, W:], gmvg_ref[...])
    G = GM_CHUNK
    g_causal = lax.broadcasted_iota(jnp.int32, (G, G), 1) <= lax.broadcasted_iota(jnp.int32, (G, G), 0)
    n_gm = tile // G
    for h in range(HEADS):
        cs = slice(h * D, (h + 1) * D)
        w_h = _mx(jnp.where(g_causal, ws_ref[h], 0.0))
        b_h = bst_ref[:, h:h + 1]
        for c0 in range(0, n_gm, 2):
            cc = list(range(c0, min(c0 + 2, n_gm)))
            vv = jnp.concatenate([_mx(v_gm[c * G:(c + 1) * G, cs]) for c in cc], axis=1)
            mixed = jnp.dot(w_h, vv, preferred_element_type=F32) + b_h
            for i, c in enumerate(cc):
                rs = slice(c * G, (c + 1) * G)
                ygm_s[rs, cs] = u[rs, cs] * mixed[:, i * D:(i + 1) * D]
    acc = x + jnp.dot(_mx(ygm_s[...]), wout_ref[0:W, :], preferred_element_type=F32)

    pairs = [(c, h) for c in range(n_ch) for h in range(HEADS)]

    def scores(c, h):
        return _dot_nt(q[h][chunks[c]], k[h][chunks[c]])

    Ca = [ca_s[h] for h in range(HEADS)]
    s_next = scores(*pairs[0])
    for i, (c, h) in enumerate(pairs):
        rs = chunks[c]
        cs = slice(h * D, (h + 1) * D)
        s_raw = s_next
        if i + 1 < len(pairs):
            s_next = scores(*pairs[i + 1])
        decay_mat = jnp.exp(jnp.where(causal, g_r[c][h:h + 1, :] - cols[c][:, h:h + 1], NEG))
        sv = jnp.dot(_mx(s_raw * decay_mat), va[h][rs], preferred_element_type=F32)
        qc = jnp.dot(q[h][rs], _mx(Ca[h]), preferred_element_type=F32)
        w_inter = cols[c][:, R + h:R + h + 1]
        floor = cols[c][:, 2 * R + h:2 * R + h + 1]
        w_s = cols[c][:, 3 * R + h:3 * R + h + 1]
        Ca[h] = decay_all[c][h:h + 1, :] * Ca[h] + jnp.dot(_mx(jnp.transpose(k[h][rs] * w_s)), va[h][rs],
                                                           preferred_element_type=F32)
        num = sv[:, :D] + w_inter * qc[:, :D]
        den = sv[:, D:] + w_inter * qc[:, D:]
        hc = num / jnp.maximum(jnp.abs(den), floor)
        hcn = _rms(hc, outg_ref[:, cs])
        yml_s[rs, cs] = jax.nn.sigmoid(proj[rs, 3 * W + h * D:3 * W + (h + 1) * D]) * (hcn + skip_ref[:, cs] * ca[rs, cs])
    for h in range(HEADS):
        ca_s[h] = Ca[h]

    xo_ref[0] = acc + jnp.dot(_mx(yml_s[...]), wout_ref[W:2 * W, :], preferred_element_type=F32)

    @pl.when(pl.program_id(1) == pl.num_programs(1) - 1)
    def _emit_state():
        for h in range(HEADS):
            c_ref[0, h] = ca_s[h, :, :D]
            n_ref[0, h:h + 1, :] = jnp.transpose(ca_s[h, :, D:])[0:1, :]
        m_ref[0] = m_s[0:HEADS, :]


def _mix(x, p):
    B, S, Dm = x.shape
    W = HEADS * HEAD_DIM
    tile = min(S, MIX_TILE)
    assert S % tile == 0 and tile % GM_CHUNK == 0 and tile % min(tile, ML_CHUNK) == 0
    weights = (p["g_mix"], p["w_in"], p["w_gt"], p["gb_col"], p["gm_v_g"], p["gm_ws"],
               p["gm_bst"], p["conv_w"], p["conv_b"], p["wqk"], p["wv"], p["out_g"], p["skip"], p["w_out"])
    return pl.pallas_call(
        functools.partial(_mix_kernel, tile=tile, width=W),
        grid=(B, S // tile),
        in_specs=[pl.BlockSpec((1, tile, Dm), lambda b, s: (b, s, 0))] + [_spec(w) for w in weights],
        out_specs=[pl.BlockSpec((1, tile, Dm), lambda b, s: (b, s, 0)),
                   pl.BlockSpec((1, HEADS, HEAD_DIM, HEAD_DIM), lambda b, s: (b, 0, 0, 0)),
                   pl.BlockSpec((1, HEADS, HEAD_DIM), lambda b, s: (b, 0, 0)),
                   pl.BlockSpec((1, HEADS, LANES), lambda b, s: (b, 0, 0)),
                   pl.BlockSpec((1, ML_CONV - 1, W), lambda b, s: (b, 0, 0))],
        out_shape=[jax.ShapeDtypeStruct((B, S, Dm), F32),
                   jax.ShapeDtypeStruct((B, HEADS, HEAD_DIM, HEAD_DIM), F32),
                   jax.ShapeDtypeStruct((B, HEADS, HEAD_DIM), F32),
                   jax.ShapeDtypeStruct((B, HEADS, LANES), F32),
                   jax.ShapeDtypeStruct((B, ML_CONV - 1, W), F32)],
        scratch_shapes=[pltpu.VMEM((tile + SUBLANES, W), F32), pltpu.VMEM((tile, W), F32), pltpu.VMEM((tile, W), F32),
                        pltpu.VMEM((HEADS, HEAD_DIM, 2 * HEAD_DIM), F32), pltpu.VMEM((SUBLANES, LANES), F32)],
        compiler_params=_cparams("parallel", "arbitrary"),
        name="mix",
    )(x, *map(_arr, weights))


def _xattn_kernel(x_ref, g_ref, wq_ref, k_ref, v_ref, wo_ref, o_ref, *, heads):
    x = x_ref[0]
    hq = jnp.dot(_mx(_rms(x, g_ref[...])), wq_ref[...], preferred_element_type=F32)
    dh = x.shape[1] // heads
    scale = dh ** -0.5
    cols = [slice(h * dh, (h + 1) * dh) for h in range(heads)]

    def scores(h):
        return _dot_nt(hq[:, cols[h]], k_ref[0, :, cols[h]]) * scale

    acc = x
    pairs = [list(range(h0, min(h0 + 2, heads))) for h0 in range(0, heads, 2)]
    sc_next = [scores(h) for h in pairs[0]]
    for i, hs in enumerate(pairs):
        sc = sc_next
        if i + 1 < len(pairs):
            sc_next = [scores(h) for h in pairs[i + 1]]
        es = [jnp.exp(s_ - jnp.max(s_, axis=-1, keepdims=True)) for s_ in sc]
        probs = [e / jnp.sum(e, axis=-1, keepdims=True) for e in es]
        att = jnp.concatenate([_dot(p_, v_ref[0, :, cols[h]]) for p_, h in zip(probs, hs)], axis=-1)
        acc = acc + jnp.dot(_mx(att), wo_ref[hs[0] * dh:(hs[-1] + 1) * dh, :], preferred_element_type=F32)
    o_ref[0] = acc


def _xattn(x, g, wq, mk, mv, wo, layer):
    B, S, Dm = x.shape
    Mt = mk.shape[1]
    kv = pl.BlockSpec((1, Mt, Dm), lambda b, s: (layer * B + b, 0, 0))
    tile = min(S, XA_TILE)
    assert S % tile == 0
    return pl.pallas_call(
        functools.partial(_xattn_kernel, heads=HEADS),
        grid=(B, S // tile),
        in_specs=[pl.BlockSpec((1, tile, Dm), lambda b, s: (b, s, 0)), _full_spec(g.shape), _spec(wq), kv, kv, _spec(wo)],
        out_specs=pl.BlockSpec((1, tile, Dm), lambda b, s: (b, s, 0)),
        out_shape=jax.ShapeDtypeStruct((B, S, Dm), F32),
        compiler_params=_cparams("parallel", "parallel"),
        name="xattn",
    )(x, g, _arr(wq), mk, mv, _arr(wo))


def _ffn_kernel(x_ref, g_ref, wu_ref, wd_ref, gf_ref, o_ref, *, final):
    x = x_ref[...]
    hn = _mx(_rms(x, g_ref[...]))
    acc = x
    dff = wu_ref.shape[1]
    fc = min(dff, FFN_CHUNK)
    for c in range(dff // fc):
        up = jnp.dot(hn, wu_ref[:, c * fc:(c + 1) * fc], preferred_element_type=F32)
        act = jnp.square(jnp.maximum(up, 0.0))
        acc = acc + jnp.dot(_mx(act), wd_ref[c * fc:(c + 1) * fc, :], preferred_element_type=F32)
    o_ref[...] = _rms(acc, gf_ref[...]) if final else acc


def _ffn(x2d, g, wu, wd, gf, final):
    M, Dm = x2d.shape
    tile = min(M, FFN_TILE)
    assert M % tile == 0
    return pl.pallas_call(
        functools.partial(_ffn_kernel, final=final),
        grid=(M // tile,),
        in_specs=[pl.BlockSpec((tile, Dm), lambda i: (i, 0)), _full_spec(g.shape), _spec(wu), _spec(wd), _full_spec(gf.shape)],
        out_specs=pl.BlockSpec((tile, Dm), lambda i: (i, 0)),
        out_shape=jax.ShapeDtypeStruct((M, Dm), F32),
        compiler_params=_cparams("parallel"),
        name="ffn",
    )(x2d, g, _arr(wu), _arr(wd), gf)


def _s_front_kernel(x_ref, gmix_ref, win_ref, wg_ref, gbrow_ref, gmvg_ref, ws0_ref, bs0_ref, cw_ref, cb_ref,
                    cst_ref, wq_ref, wk_ref, wv_ref, wqt_ref, wkt_ref, n0_ref, m0_ref, skip_ref,
                    ygm_ref, gmv_ref, cnew_ref, qt_ref, kt_ref, vs_ref, a_ref, bc_ref, nn_ref, mt_ref, wi_ref,
                    og_ref, sc_ref, *, width):
    W = width
    D = HEAD_DIM
    x = x_ref[...]
    hn = _mx(_rms(x, gmix_ref[...]))
    proj = jnp.dot(hn, win_ref[:, :4 * W], preferred_element_type=F32)
    g_col = jnp.dot(hn, wg_ref[...], preferred_element_type=F32) + gbrow_ref[...]
    ig = g_col[:, :LANES]
    lf = _log_sigmoid(g_col[:, LANES:])

    z = jax.nn.gelu(proj[:, :2 * W])
    v_gm = _rms(z[:, W:], gmvg_ref[...])
    gmv_ref[...] = v_gm
    ygm_ref[...] = z[:, :W] * (ws0_ref[...] * v_gm + bs0_ref[...])

    xm = proj[:, 2 * W:3 * W]
    conv = cb_ref[...] + cw_ref[ML_CONV - 1:ML_CONV, :] * xm
    for j in range(ML_CONV - 1):
        conv = conv + cw_ref[j:j + 1, :] * cst_ref[:, j * W:(j + 1) * W]
    ca = conv * jax.nn.sigmoid(conv)
    cnew_ref[:, :(ML_CONV - 2) * W] = cst_ref[:, W:]
    cnew_ref[:, (ML_CONV - 2) * W:] = xm
    og_ref[...] = jax.nn.sigmoid(proj[:, 3 * W:])
    sc_ref[...] = skip_ref[...] * ca

    a = lf + m0_ref[...]
    mt = jnp.maximum(a, ig)
    w_inter = jnp.exp(a - mt)
    w_in_gate = jnp.exp(ig - mt)
    floor = jnp.exp(-mt)
    mt_ref[...] = mt
    wi_ref[...] = w_inter
    k_scale = D ** -0.5
    for h in range(HEADS):
        cs = slice(h * D, (h + 1) * D)
        ca_h = _mx(ca[:, cs])
        q = jnp.dot(ca_h, wq_ref[h], preferred_element_type=F32)
        k = jnp.dot(ca_h, wk_ref[h], preferred_element_type=F32) * k_scale
        v = jnp.dot(_mx(xm[:, cs]), wv_ref[h], preferred_element_type=F32)
        qt_ref[h] = _dot_nt(wqt_ref[h], ca_h)
        kt_ref[h] = _dot_nt(wkt_ref[h], ca_h) * k_scale
        wi_h = w_inter[:, h:h + 1]
        wg_h = w_in_gate[:, h:h + 1]
        n0 = n0_ref[:, cs]
        s = jnp.sum(q * k, axis=1, keepdims=True) * wg_h
        den = s + wi_h * jnp.sum(q * n0, axis=1, keepdims=True)
        dnm = jnp.maximum(jnp.abs(den), floor[:, h:h + 1])
        a_ref[:, cs] = s * v / dnm
        bc_ref[:, cs] = jnp.broadcast_to(wi_h / dnm, (x.shape[0], D))
        vs_ref[:, cs] = wg_h * v
        nn_ref[:, cs] = wi_h * n0 + wg_h * k


def _s_front(x, cst, n0, m0p, p):
    N, Dm = x.shape
    W = HEADS * HEAD_DIM
    ins = (x, p["g_mix"], p["w_in"], p["w_g"], p["gb_row"], p["gm_v_g"], p["gm_ws0"], p["gm_bs0"],
           p["conv_w"], p["conv_b"], cst, p["wq"], p["wk"], p["wv"], p["wqt"], p["wkt"], n0, m0p, p["skip"])
    row = jax.ShapeDtypeStruct((N, W), F32)
    colw = jax.ShapeDtypeStruct((N, LANES), F32)
    tr = jax.ShapeDtypeStruct((HEADS, HEAD_DIM, N), F32)
    outs = [row, row, jax.ShapeDtypeStruct((N, (ML_CONV - 1) * W), F32), tr, tr, row, row, row, row, colw, colw, row, row]
    return pl.pallas_call(
        functools.partial(_s_front_kernel, width=W),
        grid=(1,),
        in_specs=[_spec(a) for a in ins],
        out_specs=[_full_spec(o.shape) for o in outs],
        out_shape=outs,
        compiler_params=_cparams("arbitrary"),
        name="s_front",
    )(*map(_arr, ins))


def _s_state_kernel(dec_ref, c_ref, qt_ref, kt_ref, vs_ref, *rest, block, nseq):
    cn_ref, qc_ref = rest[-2:]
    D = HEAD_DIM

    @pl.when(pl.program_id(0) == 0)
    def _update():
        base = pl.program_id(1) * block
        shift = (nseq - base) % nseq
        for h in range(HEADS):
            qt = pltpu.roll(qt_ref[h], shift, 1)
            kt = pltpu.roll(kt_ref[h], shift, 1)
            for j in range(block):
                C = c_ref[0, j, h]
                qc_ref[j:j + 1, h * D:(h + 1) * D] = jnp.sum(C * qt[:, j:j + 1], axis=0, keepdims=True)
                decay = dec_ref[(base + j) * HEADS + h]
                cn_ref[0, j, h] = decay * C + kt[:, j:j + 1] * vs_ref[j:j + 1, h * D:(h + 1) * D]

    @pl.when(pl.program_id(0) > 0)
    def _fill():
        cn_ref[...] = jnp.zeros_like(cn_ref)


def _s_state(dec, C_all, qt, kt, vs, layer, C_new_all):
    depth, N = C_all.shape[:2]
    W = HEADS * HEAD_DIM
    bb = min(N, STATE_BLOCK)
    assert N % bb == 0
    nblk = N // bb
    passes = depth if C_new_all is None else 1
    blk = lambda p, i: jnp.where(p == 0, i, nblk - 1)
    cin = pl.BlockSpec((1, bb, HEADS, HEAD_DIM, HEAD_DIM), lambda p, i: (layer, blk(p, i), 0, 0, 0))
    cout = pl.BlockSpec((1, bb, HEADS, HEAD_DIM, HEAD_DIM), lambda p, i: ((layer + p) % depth, i, 0, 0, 0))
    rblk = pl.BlockSpec((bb, W), lambda p, i: (blk(p, i), 0))
    ins = [dec, C_all, qt, kt, vs]
    in_specs = [pl.BlockSpec(memory_space=pltpu.SMEM), cin, _full_spec(qt.shape), _full_spec(kt.shape), rblk]
    aliases = {}
    if C_new_all is not None:
        ins.append(C_new_all)
        in_specs.append(pl.BlockSpec(memory_space=pl.ANY))
        aliases = {len(ins) - 1: 0}
    return pl.pallas_call(
        functools.partial(_s_state_kernel, block=bb, nseq=N),
        grid=(passes, nblk),
        in_specs=in_specs,
        out_specs=[cout, rblk],
        out_shape=[jax.ShapeDtypeStruct(C_all.shape, F32), jax.ShapeDtypeStruct((N, W), F32)],
        input_output_aliases=aliases,
        compiler_params=_cparams("arbitrary", "arbitrary"),
        name="s_state",
    )(*ins)


def _s_back_kernel(x_ref, ygm_ref, a_ref, bc_ref, qc_ref, og_ref, sc_ref, outg_ref, wout_ref, gxa_ref, wcq_ref,
                   x1_ref, hq_ref, *, width):
    W = width
    D = HEAD_DIM
    hc = a_ref[...] + bc_ref[...] * qc_ref[...]
    parts = [ygm_ref[...]]
    for h in range(HEADS):
        cs = slice(h * D, (h + 1) * D)
        parts.append(og_ref[:, cs] * (_rms(hc[:, cs], outg_ref[:, cs]) + sc_ref[:, cs]))
    y = jnp.concatenate(parts, axis=-1)
    x1 = x_ref[...] + jnp.dot(_mx(y), wout_ref[...], preferred_element_type=F32)
    x1_ref[...] = x1
    hq_ref[...] = jnp.dot(_mx(_rms(x1, gxa_ref[...])), wcq_ref[...], preferred_element_type=F32)


def _s_back(x, ygm, a, bc, qc, og, sc, p):
    N, Dm = x.shape
    ins = (x, ygm, a, bc, qc, og, sc, p["out_g"], p["w_out"], p["g_xa"], p["w_cq"])
    out = jax.ShapeDtypeStruct((N, Dm), F32)
    return pl.pallas_call(
        functools.partial(_s_back_kernel, width=HEADS * HEAD_DIM),
        grid=(1,),
        in_specs=[_spec(a_) for a_ in ins],
        out_specs=[_full_spec(out.shape)] * 2,
        out_shape=[out, out],
        compiler_params=_cparams("arbitrary"),
        name="s_back",
    )(*map(_arr, ins))


def _s_attn_kernel(hq_ref, k_ref, v_ref, o_ref, *, block):
    scale = hq_ref.shape[-1] ** -0.5
    for j in range(block):
        q = hq_ref[0, j]
        sc = jnp.sum(k_ref[0, j] * q[None], axis=-1, keepdims=True) * scale
        e = jnp.exp(sc - jnp.max(sc, axis=0, keepdims=True))
        prob = e / jnp.sum(e, axis=0, keepdims=True)
        o_ref[0, j] = jnp.sum(v_ref[0, j] * prob, axis=0)


def _s_attn(hq, K_all, V_all, layer):
    N, Dm = hq.shape
    _, _, Mt, H, dh = K_all.shape
    bb = min(N, STATE_BLOCK)
    assert N % bb == 0
    kv = pl.BlockSpec((1, bb, Mt, H, dh), lambda i: (layer, i, 0, 0, 0))
    qo = pl.BlockSpec((1, bb, H, dh), lambda i: (i, 0, 0, 0))
    out = pl.pallas_call(
        functools.partial(_s_attn_kernel, block=bb),
        grid=(N // bb,),
        in_specs=[qo, kv, kv],
        out_specs=qo,
        out_shape=jax.ShapeDtypeStruct((N // bb, bb, H, dh), F32),
        compiler_params=_cparams("parallel"),
        name="s_attn",
    )(hq.reshape(N // bb, bb, H, dh), K_all, V_all)
    return out.reshape(N, Dm)


def _s_post_kernel(x1_ref, att_ref, wco_ref, gffn_ref, wu_ref, wd_ref, gf_ref, o_ref, hn_s, *, final):
    i = pl.program_id(0)

    @pl.when(i == 0)
    def _first():
        x2 = x1_ref[...] + jnp.dot(_mx(att_ref[...]), wco_ref[...], preferred_element_type=F32)
        o_ref[...] = x2
        hn_s[...] = _rms(x2, gffn_ref[...])

    up = jnp.dot(_mx(hn_s[...]), wu_ref[...], preferred_element_type=F32)
    act = jnp.square(jnp.maximum(up, 0.0))
    o_ref[...] += jnp.dot(_mx(act), wd_ref[...], preferred_element_type=F32)

    if final:
        @pl.when(i == pl.num_programs(0) - 1)
        def _last():
            o_ref[...] = _rms(o_ref[...], gf_ref[...])


def _s_post(x1, att, p, gf, final):
    N, Dm = x1.shape
    dff = p["w_up"].shape[1]
    fc = min(dff, FFN_CHUNK)
    layer = p["w_up"].layer
    return pl.pallas_call(
        functools.partial(_s_post_kernel, final=final),
        grid=(dff // fc,),
        in_specs=[_full_spec((N, Dm)), _full_spec((N, Dm)), _spec(p["w_co"]), _full_spec((1, Dm)),
                  pl.BlockSpec((None, Dm, fc), lambda i: (layer, 0, i)), pl.BlockSpec((None, fc, Dm), lambda i: (layer, i, 0)),
                  _full_spec((1, Dm))],
        out_specs=_full_spec((N, Dm)),
        out_shape=jax.ShapeDtypeStruct((N, Dm), F32),
        scratch_shapes=[pltpu.VMEM((N, Dm), F32)],
        compiler_params=_cparams("arbitrary"),
        name="s_post",
    )(x1, att, _arr(p["w_co"]), p["g_ffn"], _arr(p["w_up"]), _arr(p["w_down"]), gf)


def _layer_params(l, big, norm_mix_g, w_in, gm_v_norm_g, gm_ws, gm_bs, ml_conv_w, ml_conv_b, ml_wq, ml_wk, ml_wv,
                  ml_b_i, ml_b_f, ml_out_norm_g, ml_skip, w_out, norm_mem_g, w_ck, w_cv, norm_xa_g, w_cq, w_co,
                  norm_ffn_g, w_up, w_down):
    W = HEADS * HEAD_DIM
    Dm = w_in.shape[1]
    row = lambda a: a.reshape(1, -1).astype(F32)
    wi = w_in[l]
    w_gate = wi[:, 4 * W:]
    w_g = jnp.zeros((Dm, 2 * LANES), F32).at[:, :HEADS].set(w_gate[:, :HEADS]).at[:, LANES:LANES + HEADS].set(w_gate[:, HEADS:])
    w_gt = jnp.zeros((2 * SUBLANES, Dm), F32).at[:HEADS].set(w_gate[:, :HEADS].T).at[SUBLANES:SUBLANES + HEADS].set(w_gate[:, HEADS:].T)
    gb_row = jnp.zeros((1, 2 * LANES), F32).at[0, :HEADS].set(ml_b_i[l]).at[0, LANES:LANES + HEADS].set(ml_b_f[l])
    gb_col = jnp.zeros((2 * SUBLANES, 1), F32).at[:HEADS, 0].set(ml_b_i[l]).at[SUBLANES:SUBLANES + HEADS, 0].set(ml_b_f[l])
    return {
        "g_mix": row(norm_mix_g[l]), "w_in": _LayerOf(big["w_in"], l), "w_g": _mx(w_g), "w_gt": _mx(w_gt),
        "gb_row": gb_row, "gb_col": gb_col, "gm_v_g": row(gm_v_norm_g[l]), "gm_ws": gm_ws[l], "gm_bst": gm_bs[l].T,
        "gm_ws0": row(jnp.repeat(gm_ws[l][:, 0, 0], HEAD_DIM)), "gm_bs0": row(jnp.repeat(gm_bs[l][:, 0], HEAD_DIM)),
        "conv_w": ml_conv_w[l], "conv_b": row(ml_conv_b[l]),
        "wq": _mx(ml_wq[l]), "wk": _mx(ml_wk[l]), "wv": _mx(ml_wv[l]),
        "wqk": _mx(jnp.concatenate([ml_wq[l], ml_wk[l]], axis=-1)),
        "wqt": _mx(jnp.swapaxes(ml_wq[l], 1, 2)), "wkt": _mx(jnp.swapaxes(ml_wk[l], 1, 2)),
        "out_g": row(ml_out_norm_g[l]), "skip": row(ml_skip[l]), "w_out": _LayerOf(big["w_out"], l),
        "g_xa": row(norm_xa_g[l]), "w_cq": _LayerOf(big["w_cq"], l), "w_co": _LayerOf(big["w_co"], l),
        "g_ffn": row(norm_ffn_g[l]), "w_up": _LayerOf(big["w_up"], l), "w_down": _LayerOf(big["w_down"], l),
    }


def kernel(x_prompt, x_sample, mem_prompt, cache_mem_k, cache_mem_v, state_C, state_n, state_m, state_conv, norm_mix_g, w_in, gm_v_norm_g, gm_ws, gm_bs, ml_conv_w, ml_conv_b, ml_wq, ml_wk, ml_wv, ml_b_i, ml_b_f, ml_out_norm_g, ml_skip, w_out, norm_mem_g, w_ck, w_cv, norm_xa_g, w_cq, w_co, norm_ffn_g, w_up, w_down, norm_f_g):
    depth = w_in.shape[0]
    B, S, Dm = x_prompt.shape
    N = x_sample.shape[0]
    assert x_sample.shape[1] == 1
    Mt = mem_prompt.shape[1]
    W = HEADS * HEAD_DIM
    xa_dh = Dm // HEADS
    gf = norm_f_g.reshape(1, Dm).astype(F32)

    xp = x_prompt
    xs = x_sample.reshape(N, Dm)
    mem2d = mem_prompt.reshape(B * Mt, Dm)
    mk_all, mv_all, mk_split, mv_split = _memkv(mem2d, norm_mem_g.reshape(depth, 1, Dm), _mx(w_ck), _mx(w_cv), HEADS)
    mk_rows = mk_all.reshape(depth * B, Mt, Dm)
    mv_rows = mv_all.reshape(depth * B, Mt, Dm)
    big = {"w_in": _mx(w_in), "w_out": _mx(w_out), "w_cq": _mx(w_cq), "w_co": _mx(w_co), "w_up": _mx(w_up),
           "w_down": _mx(w_down)}
    C_p, n_p, m_p, cv_p = [], [], [], []
    n_s, m_s, cv_s, gv_s = [], [], [], []
    C_s = None
    for l in range(depth):
        p = _layer_params(l, big, norm_mix_g, w_in, gm_v_norm_g, gm_ws, gm_bs, ml_conv_w, ml_conv_b, ml_wq, ml_wk, ml_wv,
                          ml_b_i, ml_b_f, ml_out_norm_g, ml_skip, w_out, norm_mem_g, w_ck, w_cv, norm_xa_g, w_cq,
                          w_co, norm_ffn_g, w_up, w_down)
        final = l == depth - 1

        xp, Cp, np_, mp, cvp = _mix(xp, p)
        xp = _xattn(xp, p["g_xa"], p["w_cq"], mk_rows, mv_rows, p["w_co"], l)
        xp = _ffn(xp.reshape(B * S, Dm), p["g_ffn"], p["w_up"], p["w_down"], gf, final).reshape(B, S, Dm)
        C_p.append(Cp)
        n_p.append(np_)
        m_p.append(mp[:, :, 0])
        cv_p.append(cvp)

        m0p = jnp.zeros((N, LANES), F32).at[:, :HEADS].set(state_m[l])
        (ygm, gmv, cnew, qt, kt, vs, a, bc, nn, mt, wi, og, sc) = _s_front(
            xs, state_conv[l].reshape(N, (ML_CONV - 1) * W), state_n[l].reshape(N, W), m0p, p)
        C_s, qc = _s_state(wi[:, :HEADS].reshape(N * HEADS), state_C, qt, kt, vs, l, C_s)
        x1, hq = _s_back(xs, ygm, a, bc, qc, og, sc, p)
        att = _s_attn(hq, cache_mem_k, cache_mem_v, l)
        xs = _s_post(x1, att, p, gf, final)
        n_s.append(nn.reshape(N, HEADS, HEAD_DIM))
        m_s.append(mt[:, :HEADS])
        cv_s.append(cnew.reshape(N, ML_CONV - 1, W))
        gv_s.append(gmv.reshape(N, 1, W))

    st = lambda a: jnp.stack(a, axis=0)
    return (xp, xs.reshape(N, 1, Dm), mk_split.reshape(depth, B, Mt, HEADS, xa_dh), mv_split.reshape(depth, B, Mt, HEADS, xa_dh),
            st(C_p), st(n_p), st(m_p), st(cv_p), C_s, st(n_s), st(m_s), st(cv_s), st(gv_s))
```

```python
import functools

import jax
import jax.numpy as jnp
from jax import lax
from jax.experimental import pallas as pl
from jax.experimental.pallas import tpu as pltpu

F32 = jnp.float32
MXU_DTYPE = jnp.bfloat16

EPS = 1e-6
NEG = -1e30
LANES = 128
SUBLANES = 8
HEADS = 4
HEAD_DIM = 128
GM_CHUNK = 128
ML_CHUNK = 256
ML_CONV = 4
MIX_TILE = 512
XA_TILE = 512
FFN_TILE = 512
FFN_CHUNK = 1024
STATE_BLOCK = 8
VMEM_LIMIT = 56 * 1024 * 1024


def _cparams(*sem):
    return pltpu.CompilerParams(dimension_semantics=sem, vmem_limit_bytes=VMEM_LIMIT)


def _mx(a):
    return a.astype(MXU_DTYPE)


def _dot(a, b):
    return jnp.dot(_mx(a), _mx(b), preferred_element_type=F32)


def _dot_nt(a, b):
    return lax.dot_general(_mx(a), _mx(b), (((1,), (1,)), ((), ())), preferred_element_type=F32)


def _rms(x, g):
    return x * lax.rsqrt(jnp.mean(x * x, axis=-1, keepdims=True) + EPS) * g


def _log_sigmoid(x):
    return jnp.minimum(x, 0.0) - jnp.log1p(jnp.exp(-jnp.abs(x)))


def _split3(x):
    hi = x.astype(MXU_DTYPE)
    r1 = x - hi.astype(F32)
    mid = r1.astype(MXU_DTYPE)
    lo = (r1 - mid.astype(F32)).astype(MXU_DTYPE)
    return hi, mid, lo


def _full_spec(shape):
    n = len(shape)
    return pl.BlockSpec(shape, lambda *_: (0,) * n)


class _LayerOf:
    def __init__(self, stacked, layer):
        self.stacked, self.layer, self.shape = stacked, layer, tuple(stacked.shape[1:])


def _spec(a):
    if isinstance(a, _LayerOf):
        n, layer = len(a.shape), a.layer
        return pl.BlockSpec((None,) + a.shape, lambda *_: (layer,) + (0,) * n)
    return _full_spec(a.shape)


def _arr(a):
    return a.stacked if isinstance(a, _LayerOf) else a


def _memkv_kernel(m_ref, g_ref, wk_ref, wv_ref, k_ref, v_ref, k4_ref, v4_ref):
    hn = _mx(_rms(m_ref[...], g_ref[0]))
    heads, dh = k4_ref.shape[2:]
    for w_ref, o_ref, o4_ref in ((wk_ref, k_ref, k4_ref), (wv_ref, v_ref, v4_ref)):
        y = jnp.dot(hn, w_ref[0], preferred_element_type=F32)
        o_ref[0] = y
        for h in range(heads):
            o4_ref[0, :, h, :] = y[:, h * dh:(h + 1) * dh]


def _memkv(mem2d, g, wk, wv, heads):
    M, D = mem2d.shape
    depth = wk.shape[0]
    tm = min(M, 512)
    wspec = pl.BlockSpec((1, D, D), lambda l, i: (l, 0, 0))
    ospec = pl.BlockSpec((1, tm, D), lambda l, i: (l, i, 0))
    o4spec = pl.BlockSpec((1, tm, heads, D // heads), lambda l, i: (l, i, 0, 0))
    rows = jax.ShapeDtypeStruct((depth, M, D), F32)
    split = jax.ShapeDtypeStruct((depth, M, heads, D // heads), F32)
    return pl.pallas_call(
        _memkv_kernel,
        grid=(depth, M // tm),
        in_specs=[pl.BlockSpec((tm, D), lambda l, i: (i, 0)), pl.BlockSpec((1, 1, D), lambda l, i: (l, 0, 0)), wspec, wspec],
        out_specs=[ospec, ospec, o4spec, o4spec],
        out_shape=[rows, rows, split, split],
        compiler_params=_cparams("parallel", "parallel"),
        name="memkv",
    )(mem2d, g, wk, wv)


def _transpose_exact(eye, rows):
    return sum(lax.dot_general(eye, p, (((1,), (1,)), ((), ())), preferred_element_type=F32) for p in _split3(rows))


def _mix_kernel(x_ref, gmix_ref, win_ref, wgt_ref, gbcol_ref, gmvg_ref, ws_ref, bst_ref,
                cw_ref, cb_ref, wqk_ref, wv_ref, outg_ref, skip_ref, wout_ref,
                xo_ref, c_ref, n_ref, m_ref, conv_ref, xe_s, ygm_s, yml_s, ca_s, m_s, *, tile, width):
    W = width
    L = min(tile, ML_CHUNK)
    D = HEAD_DIM
    R = SUBLANES

    @pl.when(pl.program_id(1) == 0)
    def _init():
        xe_s[0:SUBLANES, :] = jnp.zeros((SUBLANES, W), F32)
        ca_s[...] = jnp.zeros_like(ca_s)
        m_s[...] = jnp.zeros_like(m_s)

    x = x_ref[0]
    hn = _mx(_rms(x, gmix_ref[...]))
    g_row = _dot_nt(wgt_ref[...], hn) + gbcol_ref[...]
    ig_row = g_row[:R]
    lf_row = _log_sigmoid(g_row[R:])

    row_i = lax.broadcasted_iota(jnp.int32, (L, L), 0)
    col_i = lax.broadcasted_iota(jnp.int32, (L, L), 1)
    causal = col_i <= row_i
    triu = _mx(jnp.where(row_i <= col_i, 1.0, 0.0))
    eye = _mx(jnp.where(row_i == col_i, 1.0, 0.0))
    n_ch = tile // L
    chunks = [slice(c * L, (c + 1) * L) for c in range(n_ch)]
    b_all = jnp.concatenate([sum(jnp.dot(p, triu, preferred_element_type=F32) for p in _split3(lf_row[:, rs]))
                             for rs in chunks], axis=1)
    g_all = ig_row - b_all
    pos = lax.broadcasted_iota(jnp.int32, (R, tile), 1) % L

    NB = 2 * D
    starts = [c for lo, hi in ((2 * W, 3 * W), (0, 2 * W), (3 * W, 4 * W)) for c in range(lo, hi, NB)]
    proj_blk = []
    run_max = g_all
    step = 1
    for c0 in starts:
        proj_blk.append(jnp.dot(hn, win_ref[:, c0:c0 + NB], preferred_element_type=F32))
        if step < L:
            run_max = jnp.maximum(run_max, jnp.where(pos >= step, pltpu.roll(run_max, step, 1), NEG))
            step *= 2
    while step < L:
        run_max = jnp.maximum(run_max, jnp.where(pos >= step, pltpu.roll(run_max, step, 1), NEG))
        step *= 2
    proj = jnp.concatenate(proj_blk, axis=1)

    g_r, cols, decay_all = [], [], []
    m_prev = m_s[:, 0:1]
    for rs in chunks:
        b_r = b_all[:, rs]
        g = g_all[:, rs]
        M_r = jnp.maximum(run_max[:, rs], m_prev)
        M_last = M_r[:, L - 1:L]
        g_r.append(g)
        cols.append(_transpose_exact(eye, jnp.concatenate(
            [M_r, jnp.exp(m_prev - M_r), jnp.exp(-(b_r + M_r)), jnp.exp(g - M_last)], axis=0)))
        decay_all.append(jnp.exp(m_prev - M_last))
        m_prev = b_r[:, L - 1:L] + M_last
    m_s[...] = jnp.broadcast_to(m_prev, (R, LANES))

    xm = proj[:, :W]
    xe_s[SUBLANES:SUBLANES + tile, :] = xm
    conv = cb_ref[...] + cw_ref[ML_CONV - 1:ML_CONV, :] * xm
    for j in range(1, ML_CONV):
        conv = conv + cw_ref[ML_CONV - 1 - j:ML_CONV - j, :] * xe_s[SUBLANES - j:SUBLANES - j + tile, :]
    ca = conv * jax.nn.sigmoid(conv)
    conv_ref[0] = xe_s[SUBLANES + tile - (ML_CONV - 1):SUBLANES + tile, :]
    xe_s[0:SUBLANES, :] = xe_s[tile:tile + SUBLANES, :]
    ones_blk = jnp.ones((tile, D), F32)
    k_scale = D ** -0.5
    q, k, va = [], [], []
    for h in range(HEADS):
        cs = slice(h * D, (h + 1) * D)
        qk = jnp.dot(_mx(ca[:, cs]), wqk_ref[h], preferred_element_type=F32)
        q.append(_mx(qk[:, :D]))
        k.append(qk[:, D:] * k_scale)
        v = jnp.dot(_mx(xm[:, cs]), wv_ref[h], preferred_element_type=F32)
        va.append(_mx(jnp.concatenate([v, ones_blk], axis=1)))

    z = jax.nn.gelu(proj[:, W:3 * W])
    u = z[:, :W]
    v_gm = _rms(z[:, W:], gmvg_ref[...])
    G = GM_CHUNK
    g_causal = lax.broadcasted_iota(jnp.int32, (G, G), 1) <= lax.broadcasted_iota(jnp.int32, (G, G), 0)
    n_gm = tile // G
    for h in range(HEADS):
        cs = slice(h * D, (h + 1) * D)
        w_h = _mx(jnp.where(g_causal, ws_ref[h], 0.0))
        b_h = bst_ref[:, h:h + 1]
        for c0 in range(0, n_gm, 2):
            cc = list(range(c0, min(c0 + 2, n_gm)))
            vv = jnp.concatenate([_mx(v_gm[c * G:(c + 1) * G, cs]) for c in cc], axis=1)
            mixed = jnp.dot(w_h, vv, preferred_element_type=F32) + b_h
            for i, c in enumerate(cc):
                rs = slice(c * G, (c + 1) * G)
                ygm_s[rs, cs] = u[rs, cs] * mixed[:, i * D:(i + 1) * D]
    acc = x + jnp.dot(_mx(ygm_s[...]), wout_ref[0:W, :], preferred_element_type=F32)

    pairs = [(c, h) for c in range(n_ch) for h in range(HEADS)]

    def scores(c, h):
        return _dot_nt(q[h][chunks[c]], k[h][chunks[c]])

    Ca = [ca_s[h] for h in range(HEADS)]
    s_next = scores(*pairs[0])
    for i, (c, h) in enumerate(pairs):
        rs = chunks[c]
        cs = slice(h * D, (h + 1) * D)
        s_raw = s_next
        if i + 1 < len(pairs):
            s_next = scores(*pairs[i + 1])
        decay_mat = jnp.exp(jnp.where(causal, g_r[c][h:h + 1, :] - cols[c][:, h:h + 1], NEG))
        sv = jnp.dot(_mx(s_raw * decay_mat), va[h][rs], preferred_element_type=F32)
        qc = jnp.dot(q[h][rs], _mx(Ca[h]), preferred_element_type=F32)
        w_inter = cols[c][:, R + h:R + h + 1]
        floor = cols[c][:, 2 * R + h:2 * R + h + 1]
        w_s = cols[c][:, 3 * R + h:3 * R + h + 1]
        Ca[h] = decay_all[c][h:h + 1, :] * Ca[h] + jnp.dot(_mx(jnp.transpose(k[h][rs] * w_s)), va[h][rs],
                                                           preferred_element_type=F32)
        num = sv[:, :D] + w_inter * qc[:, :D]
        den = sv[:, D:] + w_inter * qc[:, D:]
        hc = num / jnp.maximum(jnp.abs(den), floor)
        hcn = _rms(hc, outg_ref[:, cs])
        yml_s[rs, cs] = jax.nn.sigmoid(proj[rs, 3 * W + h * D:3 * W + (h + 1) * D]) * (hcn + skip_ref[:, cs] * ca[rs, cs])
    for h in range(HEADS):
        ca_s[h] = Ca[h]

    xo_ref[0] = acc + jnp.dot(_mx(yml_s[...]), wout_ref[W:2 * W, :], preferred_element_type=F32)

    @pl.when(pl.program_id(1) == pl.num_programs(1) - 1)
    def _emit_state():
        for h in range(HEADS):
            c_ref[0, h] = ca_s[h, :, :D]
            n_ref[0, h:h + 1, :] = jnp.transpose(ca_s[h, :, D:])[0:1, :]
        m_ref[0] = m_s[0:HEADS, :]


def _mix(x, p):
    B, S, Dm = x.shape
    W = HEADS * HEAD_DIM
    tile = min(S, MIX_TILE)
    assert S % tile == 0 and tile % GM_CHUNK == 0 and tile % min(tile, ML_CHUNK) == 0
    weights = (p["g_mix"], p["w_in"], p["w_gt"], p["gb_col"], p["gm_v_g"], p["gm_ws"],
               p["gm_bst"], p["conv_w"], p["conv_b"], p["wqk"], p["wv"], p["out_g"], p["skip"], p["w_out"])
    return pl.pallas_call(
        functools.partial(_mix_kernel, tile=tile, width=W),
        grid=(B, S // tile),
        in_specs=[pl.BlockSpec((1, tile, Dm), lambda b, s: (b, s, 0))] + [_spec(w) for w in weights],
        out_specs=[pl.BlockSpec((1, tile, Dm), lambda b, s: (b, s, 0)),
                   pl.BlockSpec((1, HEADS, HEAD_DIM, HEAD_DIM), lambda b, s: (b, 0, 0, 0)),
                   pl.BlockSpec((1, HEADS, HEAD_DIM), lambda b, s: (b, 0, 0)),
                   pl.BlockSpec((1, HEADS, LANES), lambda b, s: (b, 0, 0)),
                   pl.BlockSpec((1, ML_CONV - 1, W), lambda b, s: (b, 0, 0))],
        out_shape=[jax.ShapeDtypeStruct((B, S, Dm), F32),
                   jax.ShapeDtypeStruct((B, HEADS, HEAD_DIM, HEAD_DIM), F32),
                   jax.ShapeDtypeStruct((B, HEADS, HEAD_DIM), F32),
                   jax.ShapeDtypeStruct((B, HEADS, LANES), F32),
                   jax.ShapeDtypeStruct((B, ML_CONV - 1, W), F32)],
        scratch_shapes=[pltpu.VMEM((tile + SUBLANES, W), F32), pltpu.VMEM((tile, W), F32), pltpu.VMEM((tile, W), F32),
                        pltpu.VMEM((HEADS, HEAD_DIM, 2 * HEAD_DIM), F32), pltpu.VMEM((SUBLANES, LANES), F32)],
        compiler_params=_cparams("parallel", "arbitrary"),
        name="mix",
    )(x, *map(_arr, weights))


def _xattn_kernel(x_ref, g_ref, wq_ref, k_ref, v_ref, wo_ref, o_ref, *, heads):
    x = x_ref[0]
    hq = jnp.dot(_mx(_rms(x, g_ref[...])), wq_ref[...], preferred_element_type=F32)
    dh = x.shape[1] // heads
    scale = dh ** -0.5
    cols = [slice(h * dh, (h + 1) * dh) for h in range(heads)]

    def scores(h):
        return _dot_nt(hq[:, cols[h]], k_ref[0, :, cols[h]]) * scale

    acc = x
    pairs = [list(range(h0, min(h0 + 2, heads))) for h0 in range(0, heads, 2)]
    sc_next = [scores(h) for h in pairs[0]]
    for i, hs in enumerate(pairs):
        sc = sc_next
        if i + 1 < len(pairs):
            sc_next = [scores(h) for h in pairs[i + 1]]
        es = [jnp.exp(s_ - jnp.max(s_, axis=-1, keepdims=True)) for s_ in sc]
        probs = [e / jnp.sum(e, axis=-1, keepdims=True) for e in es]
        att = jnp.concatenate([_dot(p_, v_ref[0, :, cols[h]]) for p_, h in zip(probs, hs)], axis=-1)
        acc = acc + jnp.dot(_mx(att), wo_ref[hs[0] * dh:(hs[-1] + 1) * dh, :], preferred_element_type=F32)
    o_ref[0] = acc


def _xattn(x, g, wq, mk, mv, wo, layer):
    B, S, Dm = x.shape
    Mt = mk.shape[1]
    kv = pl.BlockSpec((1, Mt, Dm), lambda b, s: (layer * B + b, 0, 0))
    tile = min(S, XA_TILE)
    assert S % tile == 0
    return pl.pallas_call(
        functools.partial(_xattn_kernel, heads=HEADS),
        grid=(B, S // tile),
        in_specs=[pl.BlockSpec((1, tile, Dm), lambda b, s: (b, s, 0)), _full_spec(g.shape), _spec(wq), kv, kv, _spec(wo)],
        out_specs=pl.BlockSpec((1, tile, Dm), lambda b, s: (b, s, 0)),
        out_shape=jax.ShapeDtypeStruct((B, S, Dm), F32),
        compiler_params=_cparams("parallel", "parallel"),
        name="xattn",
    )(x, g, _arr(wq), mk, mv, _arr(wo))


def _ffn_kernel(x_ref, g_ref, wu_ref, wd_ref, gf_ref, o_ref, *, final):
    x = x_ref[...]
    hn = _mx(_rms(x, g_ref[...]))
    acc = x
    dff = wu_ref.shape[1]
    fc = min(dff, FFN_CHUNK)
    for c in range(dff // fc):
        up = jnp.dot(hn, wu_ref[:, c * fc:(c + 1) * fc], preferred_element_type=F32)
        act = jnp.square(jnp.maximum(up, 0.0))
        acc = acc + jnp.dot(_mx(act), wd_ref[c * fc:(c + 1) * fc, :], preferred_element_type=F32)
    o_ref[...] = _rms(acc, gf_ref[...]) if final else acc


def _ffn(x2d, g, wu, wd, gf, final):
    M, Dm = x2d.shape
    tile = min(M, FFN_TILE)
    assert M % tile == 0
    return pl.pallas_call(
        functools.partial(_ffn_kernel, final=final),
        grid=(M // tile,),
        in_specs=[pl.BlockSpec((tile, Dm), lambda i: (i, 0)), _full_spec(g.shape), _spec(wu), _spec(wd), _full_spec(gf.shape)],
        out_specs=pl.BlockSpec((tile, Dm), lambda i: (i, 0)),
        out_shape=jax.ShapeDtypeStruct((M, Dm), F32),
        compiler_params=_cparams("parallel"),
        name="ffn",
    )(x2d, g, _arr(wu), _arr(wd), gf)


def _s_front_kernel(x_ref, gmix_ref, win_ref, wg_ref, gbrow_ref, gmvg_ref, ws0_ref, bs0_ref, cw_ref, cb_ref,
                    cst_ref, wq_ref, wk_ref, wv_ref, wqt_ref, wkt_ref, n0_ref, m0_ref, skip_ref,
                    ygm_ref, gmv_ref, cnew_ref, qt_ref, kt_ref, vs_ref, a_ref, bc_ref, nn_ref, mt_ref, wi_ref,
                    og_ref, sc_ref, *, width):
    W = width
    D = HEAD_DIM
    x = x_ref[...]
    hn = _mx(_rms(x, gmix_ref[...]))
    proj = jnp.dot(hn, win_ref[:, :4 * W], preferred_element_type=F32)
    g_col = jnp.dot(hn, wg_ref[...], preferred_element_type=F32) + gbrow_ref[...]
    ig = g_col[:, :LANES]
    lf = _log_sigmoid(g_col[:, LANES:])

    z = jax.nn.gelu(proj[:, :2 * W])
    v_gm = _rms(z[:, W:], gmvg_ref[...])
    gmv_ref[...] = v_gm
    ygm_ref[...] = z[:, :W] * (ws0_ref[...] * v_gm + bs0_ref[...])

    xm = proj[:, 2 * W:3 * W]
    conv = cb_ref[...] + cw_ref[ML_CONV - 1:ML_CONV, :] * xm
    for j in range(ML_CONV - 1):
        conv = conv + cw_ref[j:j + 1, :] * cst_ref[:, j * W:(j + 1) * W]
    ca = conv * jax.nn.sigmoid(conv)
    cnew_ref[:, :(ML_CONV - 2) * W] = cst_ref[:, W:]
    cnew_ref[:, (ML_CONV - 2) * W:] = xm
    og_ref[...] = jax.nn.sigmoid(proj[:, 3 * W:])
    sc_ref[...] = skip_ref[...] * ca

    a = lf + m0_ref[...]
    mt = jnp.maximum(a, ig)
    w_inter = jnp.exp(a - mt)
    w_in_gate = jnp.exp(ig - mt)
    floor = jnp.exp(-mt)
    mt_ref[...] = mt
    wi_ref[...] = w_inter
    k_scale = D ** -0.5
    for h in range(HEADS):
        cs = slice(h * D, (h + 1) * D)
        ca_h = _mx(ca[:, cs])
        q = jnp.dot(ca_h, wq_ref[h], preferred_element_type=F32)
        k = jnp.dot(ca_h, wk_ref[h], preferred_element_type=F32) * k_scale
        v = jnp.dot(_mx(xm[:, cs]), wv_ref[h], preferred_element_type=F32)
        qt_ref[h] = _dot_nt(wqt_ref[h], ca_h)
        kt_ref[h] = _dot_nt(wkt_ref[h], ca_h) * k_scale
        wi_h = w_inter[:, h:h + 1]
        wg_h = w_in_gate[:, h:h + 1]
        n0 = n0_ref[:, cs]
        s = jnp.sum(q * k, axis=1, keepdims=True) * wg_h
        den = s + wi_h * jnp.sum(q * n0, axis=1, keepdims=True)
        dnm = jnp.maximum(jnp.abs(den), floor[:, h:h + 1])
        a_ref[:, cs] = s * v / dnm
        bc_ref[:, cs] = jnp.broadcast_to(wi_h / dnm, (x.shape[0], D))
        vs_ref[:, cs] = wg_h * v
        nn_ref[:, cs] = wi_h * n0 + wg_h * k


def _s_front(x, cst, n0, m0p, p):
    N, Dm = x.shape
    W = HEADS * HEAD_DIM
    ins = (x, p["g_mix"], p["w_in"], p["w_g"], p["gb_row"], p["gm_v_g"], p["gm_ws0"], p["gm_bs0"],
           p["conv_w"], p["conv_b"], cst, p["wq"], p["wk"], p["wv"], p["wqt"], p["wkt"], n0, m0p, p["skip"])
    row = jax.ShapeDtypeStruct((N, W), F32)
    colw = jax.ShapeDtypeStruct((N, LANES), F32)
    tr = jax.ShapeDtypeStruct((HEADS, HEAD_DIM, N), F32)
    outs = [row, row, jax.ShapeDtypeStruct((N, (ML_CONV - 1) * W), F32), tr, tr, row, row, row, row, colw, colw, row, row]
    return pl.pallas_call(
        functools.partial(_s_front_kernel, width=W),
        grid=(1,),
        in_specs=[_spec(a) for a in ins],
        out_specs=[_full_spec(o.shape) for o in outs],
        out_shape=outs,
        compiler_params=_cparams("arbitrary"),
        name="s_front",
    )(*map(_arr, ins))


def _s_state_kernel(dec_ref, c_ref, qt_ref, kt_ref, vs_ref, *rest, block, nseq):
    cn_ref, qc_ref = rest[-2:]
    D = HEAD_DIM

    @pl.when(pl.program_id(0) == 0)
    def _update():
        base = pl.program_id(1) * block
        shift = (nseq - base) % nseq
        for h in range(HEADS):
            qt = pltpu.roll(qt_ref[h], shift, 1)
            kt = pltpu.roll(kt_ref[h], shift, 1)
            for j in range(block):
                C = c_ref[0, j, h]
                qc_ref[j:j + 1, h * D:(h + 1) * D] = jnp.sum(C * qt[:, j:j + 1], axis=0, keepdims=True)
                decay = dec_ref[(base + j) * HEADS + h]
                cn_ref[0, j, h] = decay * C + kt[:, j:j + 1] * vs_ref[j:j + 1, h * D:(h + 1) * D]

    @pl.when(pl.program_id(0) > 0)
    def _fill():
        cn_ref[...] = jnp.zeros_like(cn_ref)


def _s_state(dec, C_all, qt, kt, vs, layer, C_new_all):
    depth, N = C_all.shape[:2]
    W = HEADS * HEAD_DIM
    bb = min(N, STATE_BLOCK)
    assert N % bb == 0
    nblk = N // bb
    passes = depth if C_new_all is None else 1
    blk = lambda p, i: jnp.where(p == 0, i, nblk - 1)
    cin = pl.BlockSpec((1, bb, HEADS, HEAD_DIM, HEAD_DIM), lambda p, i: (layer, blk(p, i), 0, 0, 0))
    cout = pl.BlockSpec((1, bb, HEADS, HEAD_DIM, HEAD_DIM), lambda p, i: ((layer + p) % depth, i, 0, 0, 0))
    rblk = pl.BlockSpec((bb, W), lambda p, i: (blk(p, i), 0))
    ins = [dec, C_all, qt, kt, vs]
    in_specs = [pl.BlockSpec(memory_space=pltpu.SMEM), cin, _full_spec(qt.shape), _full_spec(kt.shape), rblk]
    aliases = {}
    if C_new_all is not None:
        ins.append(C_new_all)
        in_specs.append(pl.BlockSpec(memory_space=pl.ANY))
        aliases = {len(ins) - 1: 0}
    return pl.pallas_call(
        functools.partial(_s_state_kernel, block=bb, nseq=N),
        grid=(passes, nblk),
        in_specs=in_specs,
        out_specs=[cout, rblk],
        out_shape=[jax.ShapeDtypeStruct(C_all.shape, F32), jax.ShapeDtypeStruct((N, W), F32)],
        input_output_aliases=aliases,
        compiler_params=_cparams("arbitrary", "arbitrary"),
        name="s_state",
    )(*ins)


def _s_back_kernel(x_ref, ygm_ref, a_ref, bc_ref, qc_ref, og_ref, sc_ref, outg_ref, wout_ref, gxa_ref, wcq_ref,
                   x1_ref, hq_ref, *, width):
    W = width
    D = HEAD_DIM
    hc = a_ref[...] + bc_ref[...] * qc_ref[...]
    parts = [ygm_ref[...]]
    for h in range(HEADS):
        cs = slice(h * D, (h + 1) * D)
        parts.append(og_ref[:, cs] * (_rms(hc[:, cs], outg_ref[:, cs]) + sc_ref[:, cs]))
    y = jnp.concatenate(parts, axis=-1)
    x1 = x_ref[...] + jnp.dot(_mx(y), wout_ref[...], preferred_element_type=F32)
    x1_ref[...] = x1
    hq_ref[...] = jnp.dot(_mx(_rms(x1, gxa_ref[...])), wcq_ref[...], preferred_element_type=F32)


def _s_back(x, ygm, a, bc, qc, og, sc, p):
    N, Dm = x.shape
    ins = (x, ygm, a, bc, qc, og, sc, p["out_g"], p["w_out"], p["g_xa"], p["w_cq"])
    out = jax.ShapeDtypeStruct((N, Dm), F32)
    return pl.pallas_call(
        functools.partial(_s_back_kernel, width=HEADS * HEAD_DIM),
        grid=(1,),
        in_specs=[_spec(a_) for a_ in ins],
        out_specs=[_full_spec(out.shape)] * 2,
        out_shape=[out, out],
        compiler_params=_cparams("arbitrary"),
        name="s_back",
    )(*map(_arr, ins))


def _s_attn_kernel(q_ref, k_ref, v_ref, o_ref, *, block, heads, scale):
    for j in range(block):
        prod = k_ref[0, j] * q_ref[0, j][None]
        prod = prod + pltpu.roll(prod, heads, 1)
        sc = jnp.sum(prod, axis=-1, keepdims=True) * scale
        e = jnp.exp(sc - jnp.max(sc, axis=0, keepdims=True))
        o_ref[0, j] = jnp.sum(v_ref[0, j] * e, axis=0) / jnp.sum(e, axis=0)


def _s_attn(hq, K_all, V_all, layer):
    N, Dm = hq.shape
    depth, _, Mt, H, dh = K_all.shape
    assert dh == 2 * LANES
    bb = min(N, STATE_BLOCK)
    assert N % bb == 0
    halves = lambda a, lead: a.reshape(lead + (H, 2, LANES)).swapaxes(-2, -3).reshape(lead + (2 * H, LANES))
    kv = pl.BlockSpec((1, bb, Mt, 2 * H, LANES), lambda i: (layer, i, 0, 0, 0))
    qo = pl.BlockSpec((1, bb, 2 * H, LANES), lambda i: (i, 0, 0, 0))
    out = pl.pallas_call(
        functools.partial(_s_attn_kernel, block=bb, heads=H, scale=dh ** -0.5),
        grid=(N // bb,),
        in_specs=[qo, kv, kv],
        out_specs=qo,
        out_shape=jax.ShapeDtypeStruct((N // bb, bb, 2 * H, LANES), F32),
        compiler_params=_cparams("parallel"),
        name="s_attn",
    )(halves(hq, (N // bb, bb)), halves(K_all, (depth, N, Mt)), halves(V_all, (depth, N, Mt)))
    return out.reshape(N, 2, H, LANES).swapaxes(1, 2).reshape(N, Dm)


def _s_post_kernel(x1_ref, att_ref, wco_ref, gffn_ref, wu_ref, wd_ref, gf_ref, o_ref, hn_s, *, final):
    i = pl.program_id(0)

    @pl.when(i == 0)
    def _first():
        x2 = x1_ref[...] + jnp.dot(_mx(att_ref[...]), wco_ref[...], preferred_element_type=F32)
        o_ref[...] = x2
        hn_s[...] = _rms(x2, gffn_ref[...])

    up = jnp.dot(_mx(hn_s[...]), wu_ref[...], preferred_element_type=F32)
    act = jnp.square(jnp.maximum(up, 0.0))
    o_ref[...] += jnp.dot(_mx(act), wd_ref[...], preferred_element_type=F32)

    if final:
        @pl.when(i == pl.num_programs(0) - 1)
        def _last():
            o_ref[...] = _rms(o_ref[...], gf_ref[...])


def _s_post(x1, att, p, gf, final):
    N, Dm = x1.shape
    dff = p["w_up"].shape[1]
    fc = min(dff, FFN_CHUNK)
    layer = p["w_up"].layer
    return pl.pallas_call(
        functools.partial(_s_post_kernel, final=final),
        grid=(dff // fc,),
        in_specs=[_full_spec((N, Dm)), _full_spec((N, Dm)), _spec(p["w_co"]), _full_spec((1, Dm)),
                  pl.BlockSpec((None, Dm, fc), lambda i: (layer, 0, i)), pl.BlockSpec((None, fc, Dm), lambda i: (layer, i, 0)),
                  _full_spec((1, Dm))],
        out_specs=_full_spec((N, Dm)),
        out_shape=jax.ShapeDtypeStruct((N, Dm), F32),
        scratch_shapes=[pltpu.VMEM((N, Dm), F32)],
        compiler_params=_cparams("arbitrary"),
        name="s_post",
    )(x1, att, _arr(p["w_co"]), p["g_ffn"], _arr(p["w_up"]), _arr(p["w_down"]), gf)


def _layer_params(l, big, norm_mix_g, w_in, gm_v_norm_g, gm_ws, gm_bs, ml_conv_w, ml_conv_b, ml_wq, ml_wk, ml_wv,
                  ml_b_i, ml_b_f, ml_out_norm_g, ml_skip, w_out, norm_mem_g, w_ck, w_cv, norm_xa_g, w_cq, w_co,
                  norm_ffn_g, w_up, w_down):
    W = HEADS * HEAD_DIM
    Dm = w_in.shape[1]
    row = lambda a: a.reshape(1, -1).astype(F32)
    wi = w_in[l]
    w_gate = wi[:, 4 * W:]
    w_g = jnp.zeros((Dm, 2 * LANES), F32).at[:, :HEADS].set(w_gate[:, :HEADS]).at[:, LANES:LANES + HEADS].set(w_gate[:, HEADS:])
    w_gt = jnp.zeros((2 * SUBLANES, Dm), F32).at[:HEADS].set(w_gate[:, :HEADS].T).at[SUBLANES:SUBLANES + HEADS].set(w_gate[:, HEADS:].T)
    gb_row = jnp.zeros((1, 2 * LANES), F32).at[0, :HEADS].set(ml_b_i[l]).at[0, LANES:LANES + HEADS].set(ml_b_f[l])
    gb_col = jnp.zeros((2 * SUBLANES, 1), F32).at[:HEADS, 0].set(ml_b_i[l]).at[SUBLANES:SUBLANES + HEADS, 0].set(ml_b_f[l])
    return {
        "g_mix": row(norm_mix_g[l]), "w_in": _LayerOf(big["w_in"], l), "w_g": _mx(w_g), "w_gt": _mx(w_gt),
        "gb_row": gb_row, "gb_col": gb_col, "gm_v_g": row(gm_v_norm_g[l]), "gm_ws": gm_ws[l], "gm_bst": gm_bs[l].T,
        "gm_ws0": row(jnp.repeat(gm_ws[l][:, 0, 0], HEAD_DIM)), "gm_bs0": row(jnp.repeat(gm_bs[l][:, 0], HEAD_DIM)),
        "conv_w": ml_conv_w[l], "conv_b": row(ml_conv_b[l]),
        "wq": _mx(ml_wq[l]), "wk": _mx(ml_wk[l]), "wv": _mx(ml_wv[l]),
        "wqk": _mx(jnp.concatenate([ml_wq[l], ml_wk[l]], axis=-1)),
        "wqt": _mx(jnp.swapaxes(ml_wq[l], 1, 2)), "wkt": _mx(jnp.swapaxes(ml_wk[l], 1, 2)),
        "out_g": row(ml_out_norm_g[l]), "skip": row(ml_skip[l]), "w_out": _LayerOf(big["w_out"], l),
        "g_xa": row(norm_xa_g[l]), "w_cq": _LayerOf(big["w_cq"], l), "w_co": _LayerOf(big["w_co"], l),
        "g_ffn": row(norm_ffn_g[l]), "w_up": _LayerOf(big["w_up"], l), "w_down": _LayerOf(big["w_down"], l),
    }


def kernel(x_prompt, x_sample, mem_prompt, cache_mem_k, cache_mem_v, state_C, state_n, state_m, state_conv, norm_mix_g, w_in, gm_v_norm_g, gm_ws, gm_bs, ml_conv_w, ml_conv_b, ml_wq, ml_wk, ml_wv, ml_b_i, ml_b_f, ml_out_norm_g, ml_skip, w_out, norm_mem_g, w_ck, w_cv, norm_xa_g, w_cq, w_co, norm_ffn_g, w_up, w_down, norm_f_g):
    depth = w_in.shape[0]
    B, S, Dm = x_prompt.shape
    N = x_sample.shape[0]
    assert x_sample.shape[1] == 1
    Mt = mem_prompt.shape[1]
    W = HEADS * HEAD_DIM
    xa_dh = Dm // HEADS
    gf = norm_f_g.reshape(1, Dm).astype(F32)

    xp = x_prompt
    xs = x_sample.reshape(N, Dm)
    mem2d = mem_prompt.reshape(B * Mt, Dm)
    mk_all, mv_all, mk_split, mv_split = _memkv(mem2d, norm_mem_g.reshape(depth, 1, Dm), _mx(w_ck), _mx(w_cv), HEADS)
    mk_rows = mk_all.reshape(depth * B, Mt, Dm)
    mv_rows = mv_all.reshape(depth * B, Mt, Dm)
    big = {"w_in": _mx(w_in), "w_out": _mx(w_out), "w_cq": _mx(w_cq), "w_co": _mx(w_co), "w_up": _mx(w_up),
           "w_down": _mx(w_down)}
    C_p, n_p, m_p, cv_p = [], [], [], []
    n_s, m_s, cv_s, gv_s = [], [], [], []
    C_s = None
    for l in range(depth):
        p = _layer_params(l, big, norm_mix_g, w_in, gm_v_norm_g, gm_ws, gm_bs, ml_conv_w, ml_conv_b, ml_wq, ml_wk, ml_wv,
                          ml_b_i, ml_b_f, ml_out_norm_g, ml_skip, w_out, norm_mem_g, w_ck, w_cv, norm_xa_g, w_cq,
                          w_co, norm_ffn_g, w_up, w_down)
        final = l == depth - 1

        xp, Cp, np_, mp, cvp = _mix(xp, p)
        xp = _xattn(xp, p["g_xa"], p["w_cq"], mk_rows, mv_rows, p["w_co"], l)
        xp = _ffn(xp.reshape(B * S, Dm), p["g_ffn"], p["w_up"], p["w_down"], gf, final).reshape(B, S, Dm)
        C_p.append(Cp)
        n_p.append(np_)
        m_p.append(mp[:, :, 0])
        cv_p.append(cvp)

        m0p = jnp.zeros((N, LANES), F32).at[:, :HEADS].set(state_m[l])
        (ygm, gmv, cnew, qt, kt, vs, a, bc, nn, mt, wi, og, sc) = _s_front(
            xs, state_conv[l].reshape(N, (ML_CONV - 1) * W), state_n[l].reshape(N, W), m0p, p)
        C_s, qc = _s_state(wi[:, :HEADS].reshape(N * HEADS), state_C, qt, kt, vs, l, C_s)
        x1, hq = _s_back(xs, ygm, a, bc, qc, og, sc, p)
        att = _s_attn(hq, cache_mem_k, cache_mem_v, l)
        xs = _s_post(x1, att, p, gf, final)
        n_s.append(nn.reshape(N, HEADS, HEAD_DIM))
        m_s.append(mt[:, :HEADS])
        cv_s.append(cnew.reshape(N, ML_CONV - 1, W))
        gv_s.append(gmv.reshape(N, 1, W))

    st = lambda a: jnp.stack(a, axis=0)
    return (xp, xs.reshape(N, 1, Dm), mk_split.reshape(depth, B, Mt, HEADS, xa_dh), mv_split.reshape(depth, B, Mt, HEADS, xa_dh),
            st(C_p), st(n_p), st(m_p), st(cv_p), C_s, st(n_s), st(m_s), st(cv_s), st(gv_s))
```

```python
import functools

import jax
import jax.numpy as jnp
from jax import lax
from jax.experimental import pallas as pl
from jax.experimental.pallas import tpu as pltpu

F32 = jnp.float32
MXU_DTYPE = jnp.bfloat16

EPS = 1e-6
NEG = -1e30
LANES = 128
SUBLANES = 8
HEADS = 4
HEAD_DIM = 128
GM_CHUNK = 128
ML_CHUNK = 256
ML_CONV = 4
MIX_TILE = 512
XA_TILE = 1024
FFN_TILE = 1024
FFN_CHUNK = 1024
STATE_BLOCK = 8
VMEM_LIMIT = 56 * 1024 * 1024


def _cparams(*sem):
    return pltpu.CompilerParams(dimension_semantics=sem, vmem_limit_bytes=VMEM_LIMIT)


def _mx(a):
    return a.astype(MXU_DTYPE)


def _dot(a, b):
    return jnp.dot(_mx(a), _mx(b), preferred_element_type=F32)


def _dot_nt(a, b):
    return lax.dot_general(_mx(a), _mx(b), (((1,), (1,)), ((), ())), preferred_element_type=F32)


def _rms(x, g):
    return x * lax.rsqrt(jnp.mean(x * x, axis=-1, keepdims=True) + EPS) * g


def _log_sigmoid(x):
    return jnp.minimum(x, 0.0) - jnp.log1p(jnp.exp(-jnp.abs(x)))


def _split3(x):
    hi = x.astype(MXU_DTYPE)
    r1 = x - hi.astype(F32)
    mid = r1.astype(MXU_DTYPE)
    lo = (r1 - mid.astype(F32)).astype(MXU_DTYPE)
    return hi, mid, lo


def _full_spec(shape):
    n = len(shape)
    return pl.BlockSpec(shape, lambda *_: (0,) * n)


class _LayerOf:
    def __init__(self, stacked, layer):
        self.stacked, self.layer, self.shape = stacked, layer, tuple(stacked.shape[1:])


def _spec(a, single_buffer=False):
    if isinstance(a, _LayerOf):
        n, layer = len(a.shape), a.layer
        mode = dict(pipeline_mode=pl.Buffered(1)) if single_buffer else {}
        return pl.BlockSpec((None,) + a.shape, lambda *_: (layer,) + (0,) * n, **mode)
    return _full_spec(a.shape)


def _arr(a):
    return a.stacked if isinstance(a, _LayerOf) else a


def _memkv_kernel(m_ref, g_ref, wk_ref, wv_ref, k_ref, v_ref, k4_ref, v4_ref):
    hn = _mx(_rms(m_ref[...], g_ref[0]))
    heads, dh = k4_ref.shape[2:]
    for w_ref, o_ref, o4_ref in ((wk_ref, k_ref, k4_ref), (wv_ref, v_ref, v4_ref)):
        y = jnp.dot(hn, w_ref[0], preferred_element_type=F32)
        o_ref[0] = y
        for h in range(heads):
            o4_ref[0, :, h, :] = y[:, h * dh:(h + 1) * dh]


def _memkv(mem2d, g, wk, wv, heads):
    M, D = mem2d.shape
    depth = wk.shape[0]
    tm = min(M, 512)
    wspec = pl.BlockSpec((1, D, D), lambda l, i: (l, 0, 0))
    ospec = pl.BlockSpec((1, tm, D), lambda l, i: (l, i, 0))
    o4spec = pl.BlockSpec((1, tm, heads, D // heads), lambda l, i: (l, i, 0, 0))
    rows = jax.ShapeDtypeStruct((depth, M, D), F32)
    split = jax.ShapeDtypeStruct((depth, M, heads, D // heads), F32)
    return pl.pallas_call(
        _memkv_kernel,
        grid=(depth, M // tm),
        in_specs=[pl.BlockSpec((tm, D), lambda l, i: (i, 0)), pl.BlockSpec((1, 1, D), lambda l, i: (l, 0, 0)), wspec, wspec],
        out_specs=[ospec, ospec, o4spec, o4spec],
        out_shape=[rows, rows, split, split],
        compiler_params=_cparams("parallel", "parallel"),
        name="memkv",
    )(mem2d, g, wk, wv)


def _transpose_exact(eye, rows):
    return sum(lax.dot_general(eye, p, (((1,), (1,)), ((), ())), preferred_element_type=F32) for p in _split3(rows))


def _mix_kernel(x_ref, gmix_ref, win_ref, wgt_ref, gbcol_ref, gmvg_ref, ws_ref, bst_ref,
                cw_ref, cb_ref, wqk_ref, wv_ref, outg_ref, skip_ref, wout_ref,
                xo_ref, c_ref, n_ref, m_ref, conv_ref, xe_s, ygm_s, yml_s, ca_s, m_s, *, tile, width):
    W = width
    L = min(tile, ML_CHUNK)
    D = HEAD_DIM
    R = SUBLANES

    @pl.when(pl.program_id(1) == 0)
    def _init():
        xe_s[0:SUBLANES, :] = jnp.zeros((SUBLANES, W), F32)
        ca_s[...] = jnp.zeros_like(ca_s)
        m_s[...] = jnp.zeros_like(m_s)

    x = x_ref[0]
    hn = _mx(_rms(x, gmix_ref[...]))
    g_row = _dot_nt(wgt_ref[...], hn) + gbcol_ref[...]
    ig_row = g_row[:R]
    lf_row = _log_sigmoid(g_row[R:])

    row_i = lax.broadcasted_iota(jnp.int32, (L, L), 0)
    col_i = lax.broadcasted_iota(jnp.int32, (L, L), 1)
    causal = col_i <= row_i
    triu = _mx(jnp.where(row_i <= col_i, 1.0, 0.0))
    eye = _mx(jnp.where(row_i == col_i, 1.0, 0.0))
    n_ch = tile // L
    chunks = [slice(c * L, (c + 1) * L) for c in range(n_ch)]
    b_all = jnp.concatenate([sum(jnp.dot(p, triu, preferred_element_type=F32) for p in _split3(lf_row[:, rs]))
                             for rs in chunks], axis=1)
    g_all = ig_row - b_all
    pos = lax.broadcasted_iota(jnp.int32, (R, tile), 1) % L

    NB = 2 * D
    starts = [c for lo, hi in ((2 * W, 3 * W), (0, 2 * W), (3 * W, 4 * W)) for c in range(lo, hi, NB)]
    proj_blk = []
    run_max = g_all
    step = 1
    for c0 in starts:
        proj_blk.append(jnp.dot(hn, win_ref[:, c0:c0 + NB], preferred_element_type=F32))
        if step < L:
            run_max = jnp.maximum(run_max, jnp.where(pos >= step, pltpu.roll(run_max, step, 1), NEG))
            step *= 2
    while step < L:
        run_max = jnp.maximum(run_max, jnp.where(pos >= step, pltpu.roll(run_max, step, 1), NEG))
        step *= 2
    proj = jnp.concatenate(proj_blk, axis=1)

    g_r, cols, decay_all = [], [], []
    m_prev = m_s[:, 0:1]
    for rs in chunks:
        b_r = b_all[:, rs]
        g = g_all[:, rs]
        M_r = jnp.maximum(run_max[:, rs], m_prev)
        M_last = M_r[:, L - 1:L]
        g_r.append(g)
        cols.append(_transpose_exact(eye, jnp.concatenate(
            [M_r, jnp.exp(m_prev - M_r), jnp.exp(-(b_r + M_r)), jnp.exp(g - M_last)], axis=0)))
        decay_all.append(jnp.exp(m_prev - M_last))
        m_prev = b_r[:, L - 1:L] + M_last
    m_s[...] = jnp.broadcast_to(m_prev, (R, LANES))

    xm = proj[:, :W]
    xe_s[SUBLANES:SUBLANES + tile, :] = xm
    conv = cb_ref[...] + cw_ref[ML_CONV - 1:ML_CONV, :] * xm
    for j in range(1, ML_CONV):
        conv = conv + cw_ref[ML_CONV - 1 - j:ML_CONV - j, :] * xe_s[SUBLANES - j:SUBLANES - j + tile, :]
    ca = conv * jax.nn.sigmoid(conv)
    conv_ref[0] = xe_s[SUBLANES + tile - (ML_CONV - 1):SUBLANES + tile, :]
    xe_s[0:SUBLANES, :] = xe_s[tile:tile + SUBLANES, :]
    ones_blk = jnp.ones((tile, D), F32)
    k_scale = D ** -0.5
    q, k, va = [], [], []
    for h in range(HEADS):
        cs = slice(h * D, (h + 1) * D)
        qk = jnp.dot(_mx(ca[:, cs]), wqk_ref[h], preferred_element_type=F32)
        q.append(_mx(qk[:, :D]))
        k.append(qk[:, D:] * k_scale)
        v = jnp.dot(_mx(xm[:, cs]), wv_ref[h], preferred_element_type=F32)
        va.append(_mx(jnp.concatenate([v, ones_blk], axis=1)))

    z = jax.nn.gelu(proj[:, W:3 * W])
    u = z[:, :W]
    v_gm = _rms(z[:, W:], gmvg_ref[...])
    G = GM_CHUNK
    g_causal = lax.broadcasted_iota(jnp.int32, (G, G), 1) <= lax.broadcasted_iota(jnp.int32, (G, G), 0)
    n_gm = tile // G
    for h in range(HEADS):
        cs = slice(h * D, (h + 1) * D)
        w_h = _mx(jnp.where(g_causal, ws_ref[h], 0.0))
        b_h = bst_ref[:, h:h + 1]
        for c0 in range(0, n_gm, 2):
            cc = list(range(c0, min(c0 + 2, n_gm)))
            vv = jnp.concatenate([_mx(v_gm[c * G:(c + 1) * G, cs]) for c in cc], axis=1)
            mixed = jnp.dot(w_h, vv, preferred_element_type=F32) + b_h
            for i, c in enumerate(cc):
                rs = slice(c * G, (c + 1) * G)
                ygm_s[rs, cs] = u[rs, cs] * mixed[:, i * D:(i + 1) * D]
    acc = x + jnp.dot(_mx(ygm_s[...]), wout_ref[0:W, :], preferred_element_type=F32)

    pairs = [(c, h) for c in range(n_ch) for h in range(HEADS)]

    def scores(c, h):
        return _dot_nt(q[h][chunks[c]], k[h][chunks[c]])

    Ca = [ca_s[h] for h in range(HEADS)]
    s_next = scores(*pairs[0])
    for i, (c, h) in enumerate(pairs):
        rs = chunks[c]
        cs = slice(h * D, (h + 1) * D)
        s_raw = s_next
        if i + 1 < len(pairs):
            s_next = scores(*pairs[i + 1])
        decay_mat = jnp.exp(jnp.where(causal, g_r[c][h:h + 1, :] - cols[c][:, h:h + 1], NEG))
        sv = jnp.dot(_mx(s_raw * decay_mat), va[h][rs], preferred_element_type=F32)
        qc = jnp.dot(q[h][rs], _mx(Ca[h]), preferred_element_type=F32)
        w_inter = cols[c][:, R + h:R + h + 1]
        floor = cols[c][:, 2 * R + h:2 * R + h + 1]
        w_s = cols[c][:, 3 * R + h:3 * R + h + 1]
        Ca[h] = decay_all[c][h:h + 1, :] * Ca[h] + jnp.dot(_mx(jnp.transpose(k[h][rs] * w_s)), va[h][rs],
                                                           preferred_element_type=F32)
        num = sv[:, :D] + w_inter * qc[:, :D]
        den = sv[:, D:] + w_inter * qc[:, D:]
        hc = num / jnp.maximum(jnp.abs(den), floor)
        hcn = _rms(hc, outg_ref[:, cs])
        yml_s[rs, cs] = jax.nn.sigmoid(proj[rs, 3 * W + h * D:3 * W + (h + 1) * D]) * (hcn + skip_ref[:, cs] * ca[rs, cs])
    for h in range(HEADS):
        ca_s[h] = Ca[h]

    xo_ref[0] = acc + jnp.dot(_mx(yml_s[...]), wout_ref[W:2 * W, :], preferred_element_type=F32)

    @pl.when(pl.program_id(1) == pl.num_programs(1) - 1)
    def _emit_state():
        for h in range(HEADS):
            c_ref[0, h] = ca_s[h, :, :D]
            n_ref[0, h:h + 1, :] = jnp.transpose(ca_s[h, :, D:])[0:1, :]
        m_ref[0] = m_s[0:HEADS, :]


def _mix(x, p):
    B, S, Dm = x.shape
    W = HEADS * HEAD_DIM
    tile = min(S, MIX_TILE)
    assert S % tile == 0 and tile % GM_CHUNK == 0 and tile % min(tile, ML_CHUNK) == 0
    weights = (p["g_mix"], p["w_in"], p["w_gt"], p["gb_col"], p["gm_v_g"], p["gm_ws"],
               p["gm_bst"], p["conv_w"], p["conv_b"], p["wqk"], p["wv"], p["out_g"], p["skip"], p["w_out"])
    return pl.pallas_call(
        functools.partial(_mix_kernel, tile=tile, width=W),
        grid=(B, S // tile),
        in_specs=[pl.BlockSpec((1, tile, Dm), lambda b, s: (b, s, 0))] + [_spec(w) for w in weights],
        out_specs=[pl.BlockSpec((1, tile, Dm), lambda b, s: (b, s, 0)),
                   pl.BlockSpec((1, HEADS, HEAD_DIM, HEAD_DIM), lambda b, s: (b, 0, 0, 0)),
                   pl.BlockSpec((1, HEADS, HEAD_DIM), lambda b, s: (b, 0, 0)),
                   pl.BlockSpec((1, HEADS, LANES), lambda b, s: (b, 0, 0)),
                   pl.BlockSpec((1, ML_CONV - 1, W), lambda b, s: (b, 0, 0))],
        out_shape=[jax.ShapeDtypeStruct((B, S, Dm), F32),
                   jax.ShapeDtypeStruct((B, HEADS, HEAD_DIM, HEAD_DIM), F32),
                   jax.ShapeDtypeStruct((B, HEADS, HEAD_DIM), F32),
                   jax.ShapeDtypeStruct((B, HEADS, LANES), F32),
                   jax.ShapeDtypeStruct((B, ML_CONV - 1, W), F32)],
        scratch_shapes=[pltpu.VMEM((tile + SUBLANES, W), F32), pltpu.VMEM((tile, W), F32), pltpu.VMEM((tile, W), F32),
                        pltpu.VMEM((HEADS, HEAD_DIM, 2 * HEAD_DIM), F32), pltpu.VMEM((SUBLANES, LANES), F32)],
        compiler_params=_cparams("parallel", "arbitrary"),
        name="mix",
    )(x, *map(_arr, weights))


def _xattn_kernel(x_ref, g_ref, wq_ref, k_ref, v_ref, wo_ref, o_ref, *, heads):
    x = x_ref[0]
    hq = jnp.dot(_mx(_rms(x, g_ref[...])), wq_ref[...], preferred_element_type=F32)
    dh = x.shape[1] // heads
    scale = dh ** -0.5
    cols = [slice(h * dh, (h + 1) * dh) for h in range(heads)]

    def scores(h):
        return _dot_nt(hq[:, cols[h]], k_ref[0, :, cols[h]]) * scale

    acc = x
    pairs = [list(range(h0, min(h0 + 2, heads))) for h0 in range(0, heads, 2)]
    sc_next = [scores(h) for h in pairs[0]]
    for i, hs in enumerate(pairs):
        sc = sc_next
        if i + 1 < len(pairs):
            sc_next = [scores(h) for h in pairs[i + 1]]
        es = [jnp.exp(s_ - jnp.max(s_, axis=-1, keepdims=True)) for s_ in sc]
        probs = [e / jnp.sum(e, axis=-1, keepdims=True) for e in es]
        att = jnp.concatenate([_dot(p_, v_ref[0, :, cols[h]]) for p_, h in zip(probs, hs)], axis=-1)
        acc = acc + jnp.dot(_mx(att), wo_ref[hs[0] * dh:(hs[-1] + 1) * dh, :], preferred_element_type=F32)
    o_ref[0] = acc


def _xattn(x, g, wq, mk, mv, wo, layer):
    B, S, Dm = x.shape
    Mt = mk.shape[1]
    kv = pl.BlockSpec((1, Mt, Dm), lambda b, s: (layer * B + b, 0, 0))
    tile = min(S, XA_TILE)
    assert S % tile == 0
    return pl.pallas_call(
        functools.partial(_xattn_kernel, heads=HEADS),
        grid=(B, S // tile),
        in_specs=[pl.BlockSpec((1, tile, Dm), lambda b, s: (b, s, 0)), _full_spec(g.shape), _spec(wq), kv, kv, _spec(wo)],
        out_specs=pl.BlockSpec((1, tile, Dm), lambda b, s: (b, s, 0)),
        out_shape=jax.ShapeDtypeStruct((B, S, Dm), F32),
        compiler_params=_cparams("parallel", "parallel"),
        name="xattn",
    )(x, g, _arr(wq), mk, mv, _arr(wo))


def _ffn_kernel(x_ref, g_ref, wu_ref, wd_ref, gf_ref, o_ref, *, final):
    x = x_ref[...]
    hn = _mx(_rms(x, g_ref[...]))
    acc = x
    dff = wu_ref.shape[1]
    fc = min(dff, FFN_CHUNK)
    for c in range(dff // fc):
        up = jnp.dot(hn, wu_ref[:, c * fc:(c + 1) * fc], preferred_element_type=F32)
        act = jnp.square(jnp.maximum(up, 0.0))
        acc = acc + jnp.dot(_mx(act), wd_ref[c * fc:(c + 1) * fc, :], preferred_element_type=F32)
    o_ref[...] = _rms(acc, gf_ref[...]) if final else acc


def _ffn(x2d, g, wu, wd, gf, final):
    M, Dm = x2d.shape
    tile = min(M, FFN_TILE)
    assert M % tile == 0
    return pl.pallas_call(
        functools.partial(_ffn_kernel, final=final),
        grid=(M // tile,),
        in_specs=[pl.BlockSpec((tile, Dm), lambda i: (i, 0)), _full_spec(g.shape), _spec(wu, True), _spec(wd, True), _full_spec(gf.shape)],
        out_specs=pl.BlockSpec((tile, Dm), lambda i: (i, 0)),
        out_shape=jax.ShapeDtypeStruct((M, Dm), F32),
        compiler_params=_cparams("parallel"),
        name="ffn",
    )(x2d, g, _arr(wu), _arr(wd), gf)


def _s_front_kernel(x_ref, gmix_ref, win_ref, wg_ref, gbrow_ref, gmvg_ref, ws0_ref, bs0_ref, cw_ref, cb_ref,
                    cst_ref, wq_ref, wk_ref, wv_ref, wqt_ref, wkt_ref, n0_ref, m0_ref, skip_ref,
                    ygm_ref, gmv_ref, cnew_ref, qt_ref, kt_ref, vs_ref, a_ref, bc_ref, nn_ref, mt_ref, wi_ref,
                    og_ref, sc_ref, *, width):
    W = width
    D = HEAD_DIM
    x = x_ref[...]
    hn = _mx(_rms(x, gmix_ref[...]))
    proj = jnp.dot(hn, win_ref[:, :4 * W], preferred_element_type=F32)
    g_col = jnp.dot(hn, wg_ref[...], preferred_element_type=F32) + gbrow_ref[...]
    ig = g_col[:, :LANES]
    lf = _log_sigmoid(g_col[:, LANES:])

    z = jax.nn.gelu(proj[:, :2 * W])
    v_gm = _rms(z[:, W:], gmvg_ref[...])
    gmv_ref[...] = v_gm
    ygm_ref[...] = z[:, :W] * (ws0_ref[...] * v_gm + bs0_ref[...])

    xm = proj[:, 2 * W:3 * W]
    conv = cb_ref[...] + cw_ref[ML_CONV - 1:ML_CONV, :] * xm
    for j in range(ML_CONV - 1):
        conv = conv + cw_ref[j:j + 1, :] * cst_ref[:, j * W:(j + 1) * W]
    ca = conv * jax.nn.sigmoid(conv)
    cnew_ref[:, :(ML_CONV - 2) * W] = cst_ref[:, W:]
    cnew_ref[:, (ML_CONV - 2) * W:] = xm
    og_ref[...] = jax.nn.sigmoid(proj[:, 3 * W:])
    sc_ref[...] = skip_ref[...] * ca

    a = lf + m0_ref[...]
    mt = jnp.maximum(a, ig)
    w_inter = jnp.exp(a - mt)
    w_in_gate = jnp.exp(ig - mt)
    floor = jnp.exp(-mt)
    mt_ref[...] = mt
    wi_ref[...] = w_inter
    k_scale = D ** -0.5
    for h in range(HEADS):
        cs = slice(h * D, (h + 1) * D)
        ca_h = _mx(ca[:, cs])
        q = jnp.dot(ca_h, wq_ref[h], preferred_element_type=F32)
        k = jnp.dot(ca_h, wk_ref[h], preferred_element_type=F32) * k_scale
        v = jnp.dot(_mx(xm[:, cs]), wv_ref[h], preferred_element_type=F32)
        qt_ref[h] = _dot_nt(wqt_ref[h], ca_h)
        kt_ref[h] = _dot_nt(wkt_ref[h], ca_h) * k_scale
        wi_h = w_inter[:, h:h + 1]
        wg_h = w_in_gate[:, h:h + 1]
        n0 = n0_ref[:, cs]
        s = jnp.sum(q * k, axis=1, keepdims=True) * wg_h
        den = s + wi_h * jnp.sum(q * n0, axis=1, keepdims=True)
        dnm = jnp.maximum(jnp.abs(den), floor[:, h:h + 1])
        a_ref[:, cs] = s * v / dnm
        bc_ref[:, cs] = jnp.broadcast_to(wi_h / dnm, (x.shape[0], D))
        vs_ref[:, cs] = wg_h * v
        nn_ref[:, cs] = wi_h * n0 + wg_h * k


def _s_front(x, cst, n0, m0p, p):
    N, Dm = x.shape
    W = HEADS * HEAD_DIM
    ins = (x, p["g_mix"], p["w_in"], p["w_g"], p["gb_row"], p["gm_v_g"], p["gm_ws0"], p["gm_bs0"],
           p["conv_w"], p["conv_b"], cst, p["wq"], p["wk"], p["wv"], p["wqt"], p["wkt"], n0, m0p, p["skip"])
    row = jax.ShapeDtypeStruct((N, W), F32)
    colw = jax.ShapeDtypeStruct((N, LANES), F32)
    tr = jax.ShapeDtypeStruct((HEADS, HEAD_DIM, N), F32)
    outs = [row, row, jax.ShapeDtypeStruct((N, (ML_CONV - 1) * W), F32), tr, tr, row, row, row, row, colw, colw, row, row]
    return pl.pallas_call(
        functools.partial(_s_front_kernel, width=W),
        grid=(1,),
        in_specs=[_spec(a) for a in ins],
        out_specs=[_full_spec(o.shape) for o in outs],
        out_shape=outs,
        compiler_params=_cparams("arbitrary"),
        name="s_front",
    )(*map(_arr, ins))


def _s_state_kernel(dec_ref, c_ref, qt_ref, kt_ref, vs_ref, *rest, block, nseq):
    cn_ref, qc_ref = rest[-2:]
    D = HEAD_DIM

    @pl.when(pl.program_id(0) == 0)
    def _update():
        base = pl.program_id(1) * block
        shift = (nseq - base) % nseq
        for h in range(HEADS):
            qt = pltpu.roll(qt_ref[h], shift, 1)
            kt = pltpu.roll(kt_ref[h], shift, 1)
            for j in range(block):
                C = c_ref[0, j, h]
                qc_ref[j:j + 1, h * D:(h + 1) * D] = jnp.sum(C * qt[:, j:j + 1], axis=0, keepdims=True)
                decay = dec_ref[(base + j) * HEADS + h]
                cn_ref[0, j, h] = decay * C + kt[:, j:j + 1] * vs_ref[j:j + 1, h * D:(h + 1) * D]

    @pl.when(pl.program_id(0) > 0)
    def _fill():
        cn_ref[...] = jnp.zeros_like(cn_ref)


def _s_state(dec, C_all, qt, kt, vs, layer, C_new_all):
    depth, N = C_all.shape[:2]
    W = HEADS * HEAD_DIM
    bb = min(N, STATE_BLOCK)
    assert N % bb == 0
    nblk = N // bb
    passes = depth if C_new_all is None else 1
    blk = lambda p, i: jnp.where(p == 0, i, nblk - 1)
    cin = pl.BlockSpec((1, bb, HEADS, HEAD_DIM, HEAD_DIM), lambda p, i: (layer, blk(p, i), 0, 0, 0))
    cout = pl.BlockSpec((1, bb, HEADS, HEAD_DIM, HEAD_DIM), lambda p, i: ((layer + p) % depth, i, 0, 0, 0))
    rblk = pl.BlockSpec((bb, W), lambda p, i: (blk(p, i), 0))
    ins = [dec, C_all, qt, kt, vs]
    in_specs = [pl.BlockSpec(memory_space=pltpu.SMEM), cin, _full_spec(qt.shape), _full_spec(kt.shape), rblk]
    aliases = {}
    if C_new_all is not None:
        ins.append(C_new_all)
        in_specs.append(pl.BlockSpec(memory_space=pl.ANY))
        aliases = {len(ins) - 1: 0}
    return pl.pallas_call(
        functools.partial(_s_state_kernel, block=bb, nseq=N),
        grid=(passes, nblk),
        in_specs=in_specs,
        out_specs=[cout, rblk],
        out_shape=[jax.ShapeDtypeStruct(C_all.shape, F32), jax.ShapeDtypeStruct((N, W), F32)],
        input_output_aliases=aliases,
        compiler_params=_cparams("arbitrary", "arbitrary"),
        name="s_state",
    )(*ins)


def _s_back_kernel(x_ref, ygm_ref, a_ref, bc_ref, qc_ref, og_ref, sc_ref, outg_ref, wout_ref, gxa_ref, wcq_ref,
                   x1_ref, hq_ref, *, width):
    W = width
    D = HEAD_DIM
    hc = a_ref[...] + bc_ref[...] * qc_ref[...]
    parts = [ygm_ref[...]]
    for h in range(HEADS):
        cs = slice(h * D, (h + 1) * D)
        parts.append(og_ref[:, cs] * (_rms(hc[:, cs], outg_ref[:, cs]) + sc_ref[:, cs]))
    y = jnp.concatenate(parts, axis=-1)
    x1 = x_ref[...] + jnp.dot(_mx(y), wout_ref[...], preferred_element_type=F32)
    x1_ref[...] = x1
    hq_ref[...] = jnp.dot(_mx(_rms(x1, gxa_ref[...])), wcq_ref[...], preferred_element_type=F32)


def _s_back(x, ygm, a, bc, qc, og, sc, p):
    N, Dm = x.shape
    ins = (x, ygm, a, bc, qc, og, sc, p["out_g"], p["w_out"], p["g_xa"], p["w_cq"])
    out = jax.ShapeDtypeStruct((N, Dm), F32)
    return pl.pallas_call(
        functools.partial(_s_back_kernel, width=HEADS * HEAD_DIM),
        grid=(1,),
        in_specs=[_spec(a_) for a_ in ins],
        out_specs=[_full_spec(out.shape)] * 2,
        out_shape=[out, out],
        compiler_params=_cparams("arbitrary"),
        name="s_back",
    )(*map(_arr, ins))


def _s_attn_kernel(q_ref, k_ref, v_ref, o_ref, *, block, heads, scale):
    for j in range(block):
        prod = k_ref[0, j] * q_ref[0, j][None]
        prod = prod + pltpu.roll(prod, heads, 1)
        sc = jnp.sum(prod, axis=-1, keepdims=True) * scale
        e = jnp.exp(sc - jnp.max(sc, axis=0, keepdims=True))
        o_ref[0, j] = jnp.sum(v_ref[0, j] * e, axis=0) / jnp.sum(e, axis=0)


def _s_attn(hq, K_all, V_all, layer):
    N, Dm = hq.shape
    depth, _, Mt, H, dh = K_all.shape
    assert dh == 2 * LANES
    bb = min(N, STATE_BLOCK)
    assert N % bb == 0
    halves = lambda a, lead: a.reshape(lead + (H, 2, LANES)).swapaxes(-2, -3).reshape(lead + (2 * H, LANES))
    kv = pl.BlockSpec((1, bb, Mt, 2 * H, LANES), lambda i: (layer, i, 0, 0, 0))
    qo = pl.BlockSpec((1, bb, 2 * H, LANES), lambda i: (i, 0, 0, 0))
    out = pl.pallas_call(
        functools.partial(_s_attn_kernel, block=bb, heads=H, scale=dh ** -0.5),
        grid=(N // bb,),
        in_specs=[qo, kv, kv],
        out_specs=qo,
        out_shape=jax.ShapeDtypeStruct((N // bb, bb, 2 * H, LANES), F32),
        compiler_params=_cparams("parallel"),
        name="s_attn",
    )(halves(hq, (N // bb, bb)), halves(K_all, (depth, N, Mt)), halves(V_all, (depth, N, Mt)))
    return out.reshape(N, 2, H, LANES).swapaxes(1, 2).reshape(N, Dm)


def _s_post_kernel(x1_ref, att_ref, wco_ref, gffn_ref, wu_ref, wd_ref, gf_ref, o_ref, hn_s, *, final):
    i = pl.program_id(0)

    @pl.when(i == 0)
    def _first():
        x2 = x1_ref[...] + jnp.dot(_mx(att_ref[...]), wco_ref[...], preferred_element_type=F32)
        o_ref[...] = x2
        hn_s[...] = _rms(x2, gffn_ref[...])

    up = jnp.dot(_mx(hn_s[...]), wu_ref[...], preferred_element_type=F32)
    act = jnp.square(jnp.maximum(up, 0.0))
    o_ref[...] += jnp.dot(_mx(act), wd_ref[...], preferred_element_type=F32)

    if final:
        @pl.when(i == pl.num_programs(0) - 1)
        def _last():
            o_ref[...] = _rms(o_ref[...], gf_ref[...])


def _s_post(x1, att, p, gf, final):
    N, Dm = x1.shape
    dff = p["w_up"].shape[1]
    fc = min(dff, FFN_CHUNK)
    layer = p["w_up"].layer
    return pl.pallas_call(
        functools.partial(_s_post_kernel, final=final),
        grid=(dff // fc,),
        in_specs=[_full_spec((N, Dm)), _full_spec((N, Dm)), _spec(p["w_co"]), _full_spec((1, Dm)),
                  pl.BlockSpec((None, Dm, fc), lambda i: (layer, 0, i)), pl.BlockSpec((None, fc, Dm), lambda i: (layer, i, 0)),
                  _full_spec((1, Dm))],
        out_specs=_full_spec((N, Dm)),
        out_shape=jax.ShapeDtypeStruct((N, Dm), F32),
        scratch_shapes=[pltpu.VMEM((N, Dm), F32)],
        compiler_params=_cparams("arbitrary"),
        name="s_post",
    )(x1, att, _arr(p["w_co"]), p["g_ffn"], _arr(p["w_up"]), _arr(p["w_down"]), gf)


def _layer_params(l, big, norm_mix_g, w_in, gm_v_norm_g, gm_ws, gm_bs, ml_conv_w, ml_conv_b, ml_wq, ml_wk, ml_wv,
                  ml_b_i, ml_b_f, ml_out_norm_g, ml_skip, w_out, norm_mem_g, w_ck, w_cv, norm_xa_g, w_cq, w_co,
                  norm_ffn_g, w_up, w_down):
    W = HEADS * HEAD_DIM
    Dm = w_in.shape[1]
    row = lambda a: a.reshape(1, -1).astype(F32)
    wi = w_in[l]
    w_gate = wi[:, 4 * W:]
    w_g = jnp.zeros((Dm, 2 * LANES), F32).at[:, :HEADS].set(w_gate[:, :HEADS]).at[:, LANES:LANES + HEADS].set(w_gate[:, HEADS:])
    w_gt = jnp.zeros((2 * SUBLANES, Dm), F32).at[:HEADS].set(w_gate[:, :HEADS].T).at[SUBLANES:SUBLANES + HEADS].set(w_gate[:, HEADS:].T)
    gb_row = jnp.zeros((1, 2 * LANES), F32).at[0, :HEADS].set(ml_b_i[l]).at[0, LANES:LANES + HEADS].set(ml_b_f[l])
    gb_col = jnp.zeros((2 * SUBLANES, 1), F32).at[:HEADS, 0].set(ml_b_i[l]).at[SUBLANES:SUBLANES + HEADS, 0].set(ml_b_f[l])
    return {
        "g_mix": row(norm_mix_g[l]), "w_in": _LayerOf(big["w_in"], l), "w_g": _mx(w_g), "w_gt": _mx(w_gt),
        "gb_row": gb_row, "gb_col": gb_col, "gm_v_g": row(gm_v_norm_g[l]), "gm_ws": gm_ws[l], "gm_bst": gm_bs[l].T,
        "gm_ws0": row(jnp.repeat(gm_ws[l][:, 0, 0], HEAD_DIM)), "gm_bs0": row(jnp.repeat(gm_bs[l][:, 0], HEAD_DIM)),
        "conv_w": ml_conv_w[l], "conv_b": row(ml_conv_b[l]),
        "wq": _mx(ml_wq[l]), "wk": _mx(ml_wk[l]), "wv": _mx(ml_wv[l]),
        "wqk": _mx(jnp.concatenate([ml_wq[l], ml_wk[l]], axis=-1)),
        "wqt": _mx(jnp.swapaxes(ml_wq[l], 1, 2)), "wkt": _mx(jnp.swapaxes(ml_wk[l], 1, 2)),
        "out_g": row(ml_out_norm_g[l]), "skip": row(ml_skip[l]), "w_out": _LayerOf(big["w_out"], l),
        "g_xa": row(norm_xa_g[l]), "w_cq": _LayerOf(big["w_cq"], l), "w_co": _LayerOf(big["w_co"], l),
        "g_ffn": row(norm_ffn_g[l]), "w_up": _LayerOf(big["w_up"], l), "w_down": _LayerOf(big["w_down"], l),
    }


def kernel(x_prompt, x_sample, mem_prompt, cache_mem_k, cache_mem_v, state_C, state_n, state_m, state_conv, norm_mix_g, w_in, gm_v_norm_g, gm_ws, gm_bs, ml_conv_w, ml_conv_b, ml_wq, ml_wk, ml_wv, ml_b_i, ml_b_f, ml_out_norm_g, ml_skip, w_out, norm_mem_g, w_ck, w_cv, norm_xa_g, w_cq, w_co, norm_ffn_g, w_up, w_down, norm_f_g):
    depth = w_in.shape[0]
    B, S, Dm = x_prompt.shape
    N = x_sample.shape[0]
    assert x_sample.shape[1] == 1
    Mt = mem_prompt.shape[1]
    W = HEADS * HEAD_DIM
    xa_dh = Dm // HEADS
    gf = norm_f_g.reshape(1, Dm).astype(F32)

    xp = x_prompt
    xs = x_sample.reshape(N, Dm)
    mem2d = mem_prompt.reshape(B * Mt, Dm)
    mk_all, mv_all, mk_split, mv_split = _memkv(mem2d, norm_mem_g.reshape(depth, 1, Dm), _mx(w_ck), _mx(w_cv), HEADS)
    mk_rows = mk_all.reshape(depth * B, Mt, Dm)
    mv_rows = mv_all.reshape(depth * B, Mt, Dm)
    big = {"w_in": _mx(w_in), "w_out": _mx(w_out), "w_cq": _mx(w_cq), "w_co": _mx(w_co), "w_up": _mx(w_up),
           "w_down": _mx(w_down)}
    C_p, n_p, m_p, cv_p = [], [], [], []
    n_s, m_s, cv_s, gv_s = [], [], [], []
    C_s = None
    for l in range(depth):
        p = _layer_params(l, big, norm_mix_g, w_in, gm_v_norm_g, gm_ws, gm_bs, ml_conv_w, ml_conv_b, ml_wq, ml_wk, ml_wv,
                          ml_b_i, ml_b_f, ml_out_norm_g, ml_skip, w_out, norm_mem_g, w_ck, w_cv, norm_xa_g, w_cq,
                          w_co, norm_ffn_g, w_up, w_down)
        final = l == depth - 1

        xp, Cp, np_, mp, cvp = _mix(xp, p)
        xp = _xattn(xp, p["g_xa"], p["w_cq"], mk_rows, mv_rows, p["w_co"], l)
        xp = _ffn(xp.reshape(B * S, Dm), p["g_ffn"], p["w_up"], p["w_down"], gf, final).reshape(B, S, Dm)
        C_p.append(Cp)
        n_p.append(np_)
        m_p.append(mp[:, :, 0])
        cv_p.append(cvp)

        m0p = jnp.zeros((N, LANES), F32).at[:, :HEADS].set(state_m[l])
        (ygm, gmv, cnew, qt, kt, vs, a, bc, nn, mt, wi, og, sc) = _s_front(
            xs, state_conv[l].reshape(N, (ML_CONV - 1) * W), state_n[l].reshape(N, W), m0p, p)
        C_s, qc = _s_state(wi[:, :HEADS].reshape(N * HEADS), state_C, qt, kt, vs, l, C_s)
        x1, hq = _s_back(xs, ygm, a, bc, qc, og, sc, p)
        att = _s_attn(hq, cache_mem_k, cache_mem_v, l)
        xs = _s_post(x1, att, p, gf, final)
        n_s.append(nn.reshape(N, HEADS, HEAD_DIM))
        m_s.append(mt[:, :HEADS])
        cv_s.append(cnew.reshape(N, ML_CONV - 1, W))
        gv_s.append(gmv.reshape(N, 1, W))

    st = lambda a: jnp.stack(a, axis=0)
    return (xp, xs.reshape(N, 1, Dm), mk_split.reshape(depth, B, Mt, HEADS, xa_dh), mv_split.reshape(depth, B, Mt, HEADS, xa_dh),
            st(C_p), st(n_p), st(m_p), st(cv_p), C_s, st(n_s), st(m_s), st(cv_s), st(gv_s))
```

```python
import functools

import jax
import jax.numpy as jnp
from jax import lax
from jax.experimental import pallas as pl
from jax.experimental.pallas import tpu as pltpu

F32 = jnp.float32
MXU_DTYPE = jnp.bfloat16

EPS = 1e-6
NEG = -1e30
LANES = 128
SUBLANES = 8
HEADS = 4
HEAD_DIM = 128
GM_CHUNK = 128
ML_CHUNK = 256
ML_CONV = 4
MIX_TILE = 1024
XA_TILE = 1024
FFN_TILE = 1024
FFN_CHUNK = 1024
STATE_BLOCK = 8
VMEM_LIMIT = 56 * 1024 * 1024


def _cparams(*sem):
    return pltpu.CompilerParams(dimension_semantics=sem, vmem_limit_bytes=VMEM_LIMIT)


def _mx(a):
    return a.astype(MXU_DTYPE)


def _dot(a, b):
    return jnp.dot(_mx(a), _mx(b), preferred_element_type=F32)


def _dot_nt(a, b):
    return lax.dot_general(_mx(a), _mx(b), (((1,), (1,)), ((), ())), preferred_element_type=F32)


def _rms(x, g):
    return x * lax.rsqrt(jnp.mean(x * x, axis=-1, keepdims=True) + EPS) * g


def _log_sigmoid(x):
    return jnp.minimum(x, 0.0) - jnp.log1p(jnp.exp(-jnp.abs(x)))


def _split3(x):
    hi = x.astype(MXU_DTYPE)
    r1 = x - hi.astype(F32)
    mid = r1.astype(MXU_DTYPE)
    lo = (r1 - mid.astype(F32)).astype(MXU_DTYPE)
    return hi, mid, lo


def _full_spec(shape):
    n = len(shape)
    return pl.BlockSpec(shape, lambda *_: (0,) * n)


class _LayerOf:
    def __init__(self, stacked, layer):
        self.stacked, self.layer, self.shape = stacked, layer, tuple(stacked.shape[1:])


def _spec(a, single_buffer=False):
    if isinstance(a, _LayerOf):
        n, layer = len(a.shape), a.layer
        mode = dict(pipeline_mode=pl.Buffered(1)) if single_buffer else {}
        return pl.BlockSpec((None,) + a.shape, lambda *_: (layer,) + (0,) * n, **mode)
    return _full_spec(a.shape)


def _arr(a):
    return a.stacked if isinstance(a, _LayerOf) else a


def _memkv_kernel(m_ref, g_ref, wk_ref, wv_ref, k_ref, v_ref, k4_ref, v4_ref):
    hn = _mx(_rms(m_ref[...], g_ref[0]))
    heads, dh = k4_ref.shape[2:]
    for w_ref, o_ref, o4_ref in ((wk_ref, k_ref, k4_ref), (wv_ref, v_ref, v4_ref)):
        y = jnp.dot(hn, w_ref[0], preferred_element_type=F32)
        o_ref[0] = y
        for h in range(heads):
            o4_ref[0, :, h, :] = y[:, h * dh:(h + 1) * dh]


def _memkv(mem2d, g, wk, wv, heads):
    M, D = mem2d.shape
    depth = wk.shape[0]
    tm = min(M, 512)
    wspec = pl.BlockSpec((1, D, D), lambda l, i: (l, 0, 0))
    ospec = pl.BlockSpec((1, tm, D), lambda l, i: (l, i, 0))
    o4spec = pl.BlockSpec((1, tm, heads, D // heads), lambda l, i: (l, i, 0, 0))
    rows = jax.ShapeDtypeStruct((depth, M, D), F32)
    split = jax.ShapeDtypeStruct((depth, M, heads, D // heads), F32)
    return pl.pallas_call(
        _memkv_kernel,
        grid=(depth, M // tm),
        in_specs=[pl.BlockSpec((tm, D), lambda l, i: (i, 0)), pl.BlockSpec((1, 1, D), lambda l, i: (l, 0, 0)), wspec, wspec],
        out_specs=[ospec, ospec, o4spec, o4spec],
        out_shape=[rows, rows, split, split],
        compiler_params=_cparams("parallel", "parallel"),
        name="memkv",
    )(mem2d, g, wk, wv)


def _transpose_exact(eye, rows):
    return sum(lax.dot_general(eye, p, (((1,), (1,)), ((), ())), preferred_element_type=F32) for p in _split3(rows))


def _mix_kernel(x_ref, gmix_ref, win_ref, wgt_ref, gbcol_ref, gmvg_ref, ws_ref, bst_ref,
                cw_ref, cb_ref, wqk_ref, wv_ref, outg_ref, skip_ref, wout_ref,
                xo_ref, c_ref, n_ref, m_ref, conv_ref, xe_s, ygm_s, yml_s, ca_s, m_s, *, tile, width):
    W = width
    L = min(tile, ML_CHUNK)
    D = HEAD_DIM
    R = SUBLANES

    @pl.when(pl.program_id(1) == 0)
    def _init():
        xe_s[0:SUBLANES, :] = jnp.zeros((SUBLANES, W), F32)
        ca_s[...] = jnp.zeros_like(ca_s)
        m_s[...] = jnp.zeros_like(m_s)

    x = x_ref[0]
    hn = _mx(_rms(x, gmix_ref[...]))
    g_row = _dot_nt(wgt_ref[...], hn) + gbcol_ref[...]
    ig_row = g_row[:R]
    lf_row = _log_sigmoid(g_row[R:])

    row_i = lax.broadcasted_iota(jnp.int32, (L, L), 0)
    col_i = lax.broadcasted_iota(jnp.int32, (L, L), 1)
    causal = col_i <= row_i
    triu = _mx(jnp.where(row_i <= col_i, 1.0, 0.0))
    eye = _mx(jnp.where(row_i == col_i, 1.0, 0.0))
    n_ch = tile // L
    chunks = [slice(c * L, (c + 1) * L) for c in range(n_ch)]
    b_all = jnp.concatenate([sum(jnp.dot(p, triu, preferred_element_type=F32) for p in _split3(lf_row[:, rs]))
                             for rs in chunks], axis=1)
    g_all = ig_row - b_all
    pos = lax.broadcasted_iota(jnp.int32, (R, tile), 1) % L

    NB = 2 * D
    starts = [c for lo, hi in ((2 * W, 3 * W), (0, 2 * W), (3 * W, 4 * W)) for c in range(lo, hi, NB)]
    proj_blk = []
    run_max = g_all
    step = 1
    for c0 in starts:
        proj_blk.append(jnp.dot(hn, win_ref[:, c0:c0 + NB], preferred_element_type=F32))
        if step < L:
            run_max = jnp.maximum(run_max, jnp.where(pos >= step, pltpu.roll(run_max, step, 1), NEG))
            step *= 2
    while step < L:
        run_max = jnp.maximum(run_max, jnp.where(pos >= step, pltpu.roll(run_max, step, 1), NEG))
        step *= 2
    proj = jnp.concatenate(proj_blk, axis=1)

    g_r, cols, decay_all = [], [], []
    m_prev = m_s[:, 0:1]
    for rs in chunks:
        b_r = b_all[:, rs]
        g = g_all[:, rs]
        M_r = jnp.maximum(run_max[:, rs], m_prev)
        M_last = M_r[:, L - 1:L]
        g_r.append(g)
        cols.append(_transpose_exact(eye, jnp.concatenate(
            [M_r, jnp.exp(m_prev - M_r), jnp.exp(-(b_r + M_r)), jnp.exp(g - M_last)], axis=0)))
        decay_all.append(jnp.exp(m_prev - M_last))
        m_prev = b_r[:, L - 1:L] + M_last
    m_s[...] = jnp.broadcast_to(m_prev, (R, LANES))

    xm = proj[:, :W]
    xe_s[SUBLANES:SUBLANES + tile, :] = xm
    conv = cb_ref[...] + cw_ref[ML_CONV - 1:ML_CONV, :] * xm
    for j in range(1, ML_CONV):
        conv = conv + cw_ref[ML_CONV - 1 - j:ML_CONV - j, :] * xe_s[SUBLANES - j:SUBLANES - j + tile, :]
    ca = conv * jax.nn.sigmoid(conv)
    conv_ref[0] = xe_s[SUBLANES + tile - (ML_CONV - 1):SUBLANES + tile, :]
    xe_s[0:SUBLANES, :] = xe_s[tile:tile + SUBLANES, :]
    ones_blk = jnp.ones((tile, D), F32)
    k_scale = D ** -0.5
    q, k, va = [], [], []
    for h in range(HEADS):
        cs = slice(h * D, (h + 1) * D)
        qk = jnp.dot(_mx(ca[:, cs]), wqk_ref[h], preferred_element_type=F32)
        q.append(_mx(qk[:, :D]))
        k.append(qk[:, D:] * k_scale)
        v = jnp.dot(_mx(xm[:, cs]), wv_ref[h], preferred_element_type=F32)
        va.append(_mx(jnp.concatenate([v, ones_blk], axis=1)))

    z = jax.nn.gelu(proj[:, W:3 * W])
    u = z[:, :W]
    v_gm = _rms(z[:, W:], gmvg_ref[...])
    G = GM_CHUNK
    g_causal = lax.broadcasted_iota(jnp.int32, (G, G), 1) <= lax.broadcasted_iota(jnp.int32, (G, G), 0)
    n_gm = tile // G
    for h in range(HEADS):
        cs = slice(h * D, (h + 1) * D)
        w_h = _mx(jnp.where(g_causal, ws_ref[h], 0.0))
        b_h = bst_ref[:, h:h + 1]
        for c0 in range(0, n_gm, 2):
            cc = list(range(c0, min(c0 + 2, n_gm)))
            vv = jnp.concatenate([_mx(v_gm[c * G:(c + 1) * G, cs]) for c in cc], axis=1)
            mixed = jnp.dot(w_h, vv, preferred_element_type=F32) + b_h
            for i, c in enumerate(cc):
                rs = slice(c * G, (c + 1) * G)
                ygm_s[rs, cs] = u[rs, cs] * mixed[:, i * D:(i + 1) * D]
    acc = x + jnp.dot(_mx(ygm_s[...]), wout_ref[0:W, :], preferred_element_type=F32)

    pairs = [(c, h) for c in range(n_ch) for h in range(HEADS)]

    def scores(c, h):
        return _dot_nt(q[h][chunks[c]], k[h][chunks[c]])

    Ca = [ca_s[h] for h in range(HEADS)]
    s_next = scores(*pairs[0])
    for i, (c, h) in enumerate(pairs):
        rs = chunks[c]
        cs = slice(h * D, (h + 1) * D)
        s_raw = s_next
        if i + 1 < len(pairs):
            s_next = scores(*pairs[i + 1])
        decay_mat = jnp.exp(jnp.where(causal, g_r[c][h:h + 1, :] - cols[c][:, h:h + 1], NEG))
        sv = jnp.dot(_mx(s_raw * decay_mat), va[h][rs], preferred_element_type=F32)
        qc = jnp.dot(q[h][rs], _mx(Ca[h]), preferred_element_type=F32)
        w_inter = cols[c][:, R + h:R + h + 1]
        floor = cols[c][:, 2 * R + h:2 * R + h + 1]
        w_s = cols[c][:, 3 * R + h:3 * R + h + 1]
        Ca[h] = decay_all[c][h:h + 1, :] * Ca[h] + jnp.dot(_mx(jnp.transpose(k[h][rs] * w_s)), va[h][rs],
                                                           preferred_element_type=F32)
        num = sv[:, :D] + w_inter * qc[:, :D]
        den = sv[:, D:] + w_inter * qc[:, D:]
        hc = num / jnp.maximum(jnp.abs(den), floor)
        hcn = _rms(hc, outg_ref[:, cs])
        yml_s[rs, cs] = jax.nn.sigmoid(proj[rs, 3 * W + h * D:3 * W + (h + 1) * D]) * (hcn + skip_ref[:, cs] * ca[rs, cs])
    for h in range(HEADS):
        ca_s[h] = Ca[h]

    xo_ref[0] = acc + jnp.dot(_mx(yml_s[...]), wout_ref[W:2 * W, :], preferred_element_type=F32)

    @pl.when(pl.program_id(1) == pl.num_programs(1) - 1)
    def _emit_state():
        for h in range(HEADS):
            c_ref[0, h] = ca_s[h, :, :D]
            n_ref[0, h:h + 1, :] = jnp.transpose(ca_s[h, :, D:])[0:1, :]
        m_ref[0] = m_s[0:HEADS, :]


def _mix(x, p):
    B, S, Dm = x.shape
    W = HEADS * HEAD_DIM
    tile = min(S, MIX_TILE)
    assert S % tile == 0 and tile % GM_CHUNK == 0 and tile % min(tile, ML_CHUNK) == 0
    weights = (p["g_mix"], p["w_in"], p["w_gt"], p["gb_col"], p["gm_v_g"], p["gm_ws"],
               p["gm_bst"], p["conv_w"], p["conv_b"], p["wqk"], p["wv"], p["out_g"], p["skip"], p["w_out"])
    return pl.pallas_call(
        functools.partial(_mix_kernel, tile=tile, width=W),
        grid=(B, S // tile),
        in_specs=[pl.BlockSpec((1, tile, Dm), lambda b, s: (b, s, 0))] + [_spec(w) for w in weights],
        out_specs=[pl.BlockSpec((1, tile, Dm), lambda b, s: (b, s, 0)),
                   pl.BlockSpec((1, HEADS, HEAD_DIM, HEAD_DIM), lambda b, s: (b, 0, 0, 0)),
                   pl.BlockSpec((1, HEADS, HEAD_DIM), lambda b, s: (b, 0, 0)),
                   pl.BlockSpec((1, HEADS, LANES), lambda b, s: (b, 0, 0)),
                   pl.BlockSpec((1, ML_CONV - 1, W), lambda b, s: (b, 0, 0))],
        out_shape=[jax.ShapeDtypeStruct((B, S, Dm), F32),
                   jax.ShapeDtypeStruct((B, HEADS, HEAD_DIM, HEAD_DIM), F32),
                   jax.ShapeDtypeStruct((B, HEADS, HEAD_DIM), F32),
                   jax.ShapeDtypeStruct((B, HEADS, LANES), F32),
                   jax.ShapeDtypeStruct((B, ML_CONV - 1, W), F32)],
        scratch_shapes=[pltpu.VMEM((tile + SUBLANES, W), F32), pltpu.VMEM((tile, W), F32), pltpu.VMEM((tile, W), F32),
                        pltpu.VMEM((HEADS, HEAD_DIM, 2 * HEAD_DIM), F32), pltpu.VMEM((SUBLANES, LANES), F32)],
        compiler_params=_cparams("parallel", "arbitrary"),
        name="mix",
    )(x, *map(_arr, weights))


def _xattn_kernel(x_ref, g_ref, wq_ref, k_ref, v_ref, wo_ref, o_ref, *, heads):
    x = x_ref[0]
    hq = jnp.dot(_mx(_rms(x, g_ref[...])), wq_ref[...], preferred_element_type=F32)
    dh = x.shape[1] // heads
    scale = dh ** -0.5
    cols = [slice(h * dh, (h + 1) * dh) for h in range(heads)]

    def scores(h):
        return _dot_nt(hq[:, cols[h]], k_ref[0, :, cols[h]]) * scale

    acc = x
    pairs = [list(range(h0, min(h0 + 2, heads))) for h0 in range(0, heads, 2)]
    sc_next = [scores(h) for h in pairs[0]]
    for i, hs in enumerate(pairs):
        sc = sc_next
        if i + 1 < len(pairs):
            sc_next = [scores(h) for h in pairs[i + 1]]
        es = [jnp.exp(s_ - jnp.max(s_, axis=-1, keepdims=True)) for s_ in sc]
        probs = [e / jnp.sum(e, axis=-1, keepdims=True) for e in es]
        att = jnp.concatenate([_dot(p_, v_ref[0, :, cols[h]]) for p_, h in zip(probs, hs)], axis=-1)
        acc = acc + jnp.dot(_mx(att), wo_ref[hs[0] * dh:(hs[-1] + 1) * dh, :], preferred_element_type=F32)
    o_ref[0] = acc


def _xattn(x, g, wq, mk, mv, wo, layer):
    B, S, Dm = x.shape
    Mt = mk.shape[1]
    kv = pl.BlockSpec((1, Mt, Dm), lambda b, s: (layer * B + b, 0, 0))
    tile = min(S, XA_TILE)
    assert S % tile == 0
    return pl.pallas_call(
        functools.partial(_xattn_kernel, heads=HEADS),
        grid=(B, S // tile),
        in_specs=[pl.BlockSpec((1, tile, Dm), lambda b, s: (b, s, 0)), _full_spec(g.shape), _spec(wq), kv, kv, _spec(wo)],
        out_specs=pl.BlockSpec((1, tile, Dm), lambda b, s: (b, s, 0)),
        out_shape=jax.ShapeDtypeStruct((B, S, Dm), F32),
        compiler_params=_cparams("parallel", "parallel"),
        name="xattn",
    )(x, g, _arr(wq), mk, mv, _arr(wo))


def _ffn_kernel(x_ref, g_ref, wu_ref, wd_ref, gf_ref, o_ref, *, final):
    x = x_ref[...]
    hn = _mx(_rms(x, g_ref[...]))
    acc = x
    dff = wu_ref.shape[1]
    fc = min(dff, FFN_CHUNK)
    for c in range(dff // fc):
        up = jnp.dot(hn, wu_ref[:, c * fc:(c + 1) * fc], preferred_element_type=F32)
        act = jnp.square(jnp.maximum(up, 0.0))
        acc = acc + jnp.dot(_mx(act), wd_ref[c * fc:(c + 1) * fc, :], preferred_element_type=F32)
    o_ref[...] = _rms(acc, gf_ref[...]) if final else acc


def _ffn(x2d, g, wu, wd, gf, final):
    M, Dm = x2d.shape
    tile = min(M, FFN_TILE)
    assert M % tile == 0
    return pl.pallas_call(
        functools.partial(_ffn_kernel, final=final),
        grid=(M // tile,),
        in_specs=[pl.BlockSpec((tile, Dm), lambda i: (i, 0)), _full_spec(g.shape), _spec(wu, True), _spec(wd, True), _full_spec(gf.shape)],
        out_specs=pl.BlockSpec((tile, Dm), lambda i: (i, 0)),
        out_shape=jax.ShapeDtypeStruct((M, Dm), F32),
        compiler_params=_cparams("parallel"),
        name="ffn",
    )(x2d, g, _arr(wu), _arr(wd), gf)


def _s_front_kernel(x_ref, gmix_ref, win_ref, wg_ref, gbrow_ref, gmvg_ref, ws0_ref, bs0_ref, cw_ref, cb_ref,
                    cst_ref, wq_ref, wk_ref, wv_ref, wqt_ref, wkt_ref, n0_ref, m0_ref, skip_ref,
                    ygm_ref, gmv_ref, cnew_ref, qt_ref, kt_ref, vs_ref, a_ref, bc_ref, nn_ref, mt_ref, wi_ref,
                    og_ref, sc_ref, *, width):
    W = width
    D = HEAD_DIM
    x = x_ref[...]
    hn = _mx(_rms(x, gmix_ref[...]))
    proj = jnp.dot(hn, win_ref[:, :4 * W], preferred_element_type=F32)
    g_col = jnp.dot(hn, wg_ref[...], preferred_element_type=F32) + gbrow_ref[...]
    ig = g_col[:, :LANES]
    lf = _log_sigmoid(g_col[:, LANES:])

    z = jax.nn.gelu(proj[:, :2 * W])
    v_gm = _rms(z[:, W:], gmvg_ref[...])
    gmv_ref[...] = v_gm
    ygm_ref[...] = z[:, :W] * (ws0_ref[...] * v_gm + bs0_ref[...])

    xm = proj[:, 2 * W:3 * W]
    conv = cb_ref[...] + cw_ref[ML_CONV - 1:ML_CONV, :] * xm
    for j in range(ML_CONV - 1):
        conv = conv + cw_ref[j:j + 1, :] * cst_ref[:, j * W:(j + 1) * W]
    ca = conv * jax.nn.sigmoid(conv)
    cnew_ref[:, :(ML_CONV - 2) * W] = cst_ref[:, W:]
    cnew_ref[:, (ML_CONV - 2) * W:] = xm
    og_ref[...] = jax.nn.sigmoid(proj[:, 3 * W:])
    sc_ref[...] = skip_ref[...] * ca

    a = lf + m0_ref[...]
    mt = jnp.maximum(a, ig)
    w_inter = jnp.exp(a - mt)
    w_in_gate = jnp.exp(ig - mt)
    floor = jnp.exp(-mt)
    mt_ref[...] = mt
    wi_ref[...] = w_inter
    k_scale = D ** -0.5
    for h in range(HEADS):
        cs = slice(h * D, (h + 1) * D)
        ca_h = _mx(ca[:, cs])
        q = jnp.dot(ca_h, wq_ref[h], preferred_element_type=F32)
        k = jnp.dot(ca_h, wk_ref[h], preferred_element_type=F32) * k_scale
        v = jnp.dot(_mx(xm[:, cs]), wv_ref[h], preferred_element_type=F32)
        qt_ref[h] = _dot_nt(wqt_ref[h], ca_h)
        kt_ref[h] = _dot_nt(wkt_ref[h], ca_h) * k_scale
        wi_h = w_inter[:, h:h + 1]
        wg_h = w_in_gate[:, h:h + 1]
        n0 = n0_ref[:, cs]
        s = jnp.sum(q * k, axis=1, keepdims=True) * wg_h
        den = s + wi_h * jnp.sum(q * n0, axis=1, keepdims=True)
        dnm = jnp.maximum(jnp.abs(den), floor[:, h:h + 1])
        a_ref[:, cs] = s * v / dnm
        bc_ref[:, cs] = jnp.broadcast_to(wi_h / dnm, (x.shape[0], D))
        vs_ref[:, cs] = wg_h * v
        nn_ref[:, cs] = wi_h * n0 + wg_h * k


def _s_front(x, cst, n0, m0p, p):
    N, Dm = x.shape
    W = HEADS * HEAD_DIM
    ins = (x, p["g_mix"], p["w_in"], p["w_g"], p["gb_row"], p["gm_v_g"], p["gm_ws0"], p["gm_bs0"],
           p["conv_w"], p["conv_b"], cst, p["wq"], p["wk"], p["wv"], p["wqt"], p["wkt"], n0, m0p, p["skip"])
    row = jax.ShapeDtypeStruct((N, W), F32)
    colw = jax.ShapeDtypeStruct((N, LANES), F32)
    tr = jax.ShapeDtypeStruct((HEADS, HEAD_DIM, N), F32)
    outs = [row, row, jax.ShapeDtypeStruct((N, (ML_CONV - 1) * W), F32), tr, tr, row, row, row, row, colw, colw, row, row]
    return pl.pallas_call(
        functools.partial(_s_front_kernel, width=W),
        grid=(1,),
        in_specs=[_spec(a) for a in ins],
        out_specs=[_full_spec(o.shape) for o in outs],
        out_shape=outs,
        compiler_params=_cparams("arbitrary"),
        name="s_front",
    )(*map(_arr, ins))


def _s_state_kernel(dec_ref, c_ref, qt_ref, kt_ref, vs_ref, *rest, block, nseq):
    cn_ref, qc_ref = rest[-2:]
    D = HEAD_DIM

    @pl.when(pl.program_id(0) == 0)
    def _update():
        base = pl.program_id(1) * block
        shift = (nseq - base) % nseq
        for h in range(HEADS):
            qt = pltpu.roll(qt_ref[h], shift, 1)
            kt = pltpu.roll(kt_ref[h], shift, 1)
            for j in range(block):
                C = c_ref[0, j, h]
                qc_ref[j:j + 1, h * D:(h + 1) * D] = jnp.sum(C * qt[:, j:j + 1], axis=0, keepdims=True)
                decay = dec_ref[(base + j) * HEADS + h]
                cn_ref[0, j, h] = decay * C + kt[:, j:j + 1] * vs_ref[j:j + 1, h * D:(h + 1) * D]

    @pl.when(pl.program_id(0) > 0)
    def _fill():
        cn_ref[...] = jnp.zeros_like(cn_ref)


def _s_state(dec, C_all, qt, kt, vs, layer, C_new_all):
    depth, N = C_all.shape[:2]
    W = HEADS * HEAD_DIM
    bb = min(N, STATE_BLOCK)
    assert N % bb == 0
    nblk = N // bb
    passes = depth if C_new_all is None else 1
    blk = lambda p, i: jnp.where(p == 0, i, nblk - 1)
    cin = pl.BlockSpec((1, bb, HEADS, HEAD_DIM, HEAD_DIM), lambda p, i: (layer, blk(p, i), 0, 0, 0))
    cout = pl.BlockSpec((1, bb, HEADS, HEAD_DIM, HEAD_DIM), lambda p, i: ((layer + p) % depth, i, 0, 0, 0))
    rblk = pl.BlockSpec((bb, W), lambda p, i: (blk(p, i), 0))
    ins = [dec, C_all, qt, kt, vs]
    in_specs = [pl.BlockSpec(memory_space=pltpu.SMEM), cin, _full_spec(qt.shape), _full_spec(kt.shape), rblk]
    aliases = {}
    if C_new_all is not None:
        ins.append(C_new_all)
        in_specs.append(pl.BlockSpec(memory_space=pl.ANY))
        aliases = {len(ins) - 1: 0}
    return pl.pallas_call(
        functools.partial(_s_state_kernel, block=bb, nseq=N),
        grid=(passes, nblk),
        in_specs=in_specs,
        out_specs=[cout, rblk],
        out_shape=[jax.ShapeDtypeStruct(C_all.shape, F32), jax.ShapeDtypeStruct((N, W), F32)],
        input_output_aliases=aliases,
        compiler_params=_cparams("arbitrary", "arbitrary"),
        name="s_state",
    )(*ins)


def _s_back_kernel(x_ref, ygm_ref, a_ref, bc_ref, qc_ref, og_ref, sc_ref, outg_ref, wout_ref, gxa_ref, wcq_ref,
                   x1_ref, hq_ref, *, width):
    W = width
    D = HEAD_DIM
    hc = a_ref[...] + bc_ref[...] * qc_ref[...]
    parts = [ygm_ref[...]]
    for h in range(HEADS):
        cs = slice(h * D, (h + 1) * D)
        parts.append(og_ref[:, cs] * (_rms(hc[:, cs], outg_ref[:, cs]) + sc_ref[:, cs]))
    y = jnp.concatenate(parts, axis=-1)
    x1 = x_ref[...] + jnp.dot(_mx(y), wout_ref[...], preferred_element_type=F32)
    x1_ref[...] = x1
    hq_ref[...] = jnp.dot(_mx(_rms(x1, gxa_ref[...])), wcq_ref[...], preferred_element_type=F32)


def _s_back(x, ygm, a, bc, qc, og, sc, p):
    N, Dm = x.shape
    ins = (x, ygm, a, bc, qc, og, sc, p["out_g"], p["w_out"], p["g_xa"], p["w_cq"])
    out = jax.ShapeDtypeStruct((N, Dm), F32)
    return pl.pallas_call(
        functools.partial(_s_back_kernel, width=HEADS * HEAD_DIM),
        grid=(1,),
        in_specs=[_spec(a_) for a_ in ins],
        out_specs=[_full_spec(out.shape)] * 2,
        out_shape=[out, out],
        compiler_params=_cparams("arbitrary"),
        name="s_back",
    )(*map(_arr, ins))


def _s_attn_kernel(q_ref, k_ref, v_ref, o_ref, *, block, heads, scale):
    for j in range(block):
        prod = k_ref[0, j] * q_ref[0, j][None]
        prod = prod + pltpu.roll(prod, heads, 1)
        sc = jnp.sum(prod, axis=-1, keepdims=True) * scale
        e = jnp.exp(sc - jnp.max(sc, axis=0, keepdims=True))
        o_ref[0, j] = jnp.sum(v_ref[0, j] * e, axis=0) / jnp.sum(e, axis=0)


def _s_attn(hq, K_all, V_all, layer):
    N, Dm = hq.shape
    depth, _, Mt, H, dh = K_all.shape
    assert dh == 2 * LANES
    bb = min(N, STATE_BLOCK)
    assert N % bb == 0
    halves = lambda a, lead: a.reshape(lead + (H, 2, LANES)).swapaxes(-2, -3).reshape(lead + (2 * H, LANES))
    kv = pl.BlockSpec((1, bb, Mt, 2 * H, LANES), lambda i: (layer, i, 0, 0, 0))
    qo = pl.BlockSpec((1, bb, 2 * H, LANES), lambda i: (i, 0, 0, 0))
    out = pl.pallas_call(
        functools.partial(_s_attn_kernel, block=bb, heads=H, scale=dh ** -0.5),
        grid=(N // bb,),
        in_specs=[qo, kv, kv],
        out_specs=qo,
        out_shape=jax.ShapeDtypeStruct((N // bb, bb, 2 * H, LANES), F32),
        compiler_params=_cparams("parallel"),
        name="s_attn",
    )(halves(hq, (N // bb, bb)), halves(K_all, (depth, N, Mt)), halves(V_all, (depth, N, Mt)))
    return out.reshape(N, 2, H, LANES).swapaxes(1, 2).reshape(N, Dm)


def _s_post_kernel(x1_ref, att_ref, wco_ref, gffn_ref, wu_ref, wd_ref, gf_ref, o_ref, hn_s, *, final):
    i = pl.program_id(0)

    @pl.when(i == 0)
    def _first():
        x2 = x1_ref[...] + jnp.dot(_mx(att_ref[...]), wco_ref[...], preferred_element_type=F32)
        o_ref[...] = x2
        hn_s[...] = _rms(x2, gffn_ref[...])

    up = jnp.dot(_mx(hn_s[...]), wu_ref[...], preferred_element_type=F32)
    act = jnp.square(jnp.maximum(up, 0.0))
    o_ref[...] += jnp.dot(_mx(act), wd_ref[...], preferred_element_type=F32)

    if final:
        @pl.when(i == pl.num_programs(0) - 1)
        def _last():
            o_ref[...] = _rms(o_ref[...], gf_ref[...])


def _s_post(x1, att, p, gf, final):
    N, Dm = x1.shape
    dff = p["w_up"].shape[1]
    fc = min(dff, FFN_CHUNK)
    layer = p["w_up"].layer
    return pl.pallas_call(
        functools.partial(_s_post_kernel, final=final),
        grid=(dff // fc,),
        in_specs=[_full_spec((N, Dm)), _full_spec((N, Dm)), _spec(p["w_co"]), _full_spec((1, Dm)),
                  pl.BlockSpec((None, Dm, fc), lambda i: (layer, 0, i)), pl.BlockSpec((None, fc, Dm), lambda i: (layer, i, 0)),
                  _full_spec((1, Dm))],
        out_specs=_full_spec((N, Dm)),
        out_shape=jax.ShapeDtypeStruct((N, Dm), F32),
        scratch_shapes=[pltpu.VMEM((N, Dm), F32)],
        compiler_params=_cparams("arbitrary"),
        name="s_post",
    )(x1, att, _arr(p["w_co"]), p["g_ffn"], _arr(p["w_up"]), _arr(p["w_down"]), gf)


def _layer_params(l, big, norm_mix_g, w_in, gm_v_norm_g, gm_ws, gm_bs, ml_conv_w, ml_conv_b, ml_wq, ml_wk, ml_wv,
                  ml_b_i, ml_b_f, ml_out_norm_g, ml_skip, w_out, norm_mem_g, w_ck, w_cv, norm_xa_g, w_cq, w_co,
                  norm_ffn_g, w_up, w_down):
    W = HEADS * HEAD_DIM
    Dm = w_in.shape[1]
    row = lambda a: a.reshape(1, -1).astype(F32)
    wi = w_in[l]
    w_gate = wi[:, 4 * W:]
    w_g = jnp.zeros((Dm, 2 * LANES), F32).at[:, :HEADS].set(w_gate[:, :HEADS]).at[:, LANES:LANES + HEADS].set(w_gate[:, HEADS:])
    w_gt = jnp.zeros((2 * SUBLANES, Dm), F32).at[:HEADS].set(w_gate[:, :HEADS].T).at[SUBLANES:SUBLANES + HEADS].set(w_gate[:, HEADS:].T)
    gb_row = jnp.zeros((1, 2 * LANES), F32).at[0, :HEADS].set(ml_b_i[l]).at[0, LANES:LANES + HEADS].set(ml_b_f[l])
    gb_col = jnp.zeros((2 * SUBLANES, 1), F32).at[:HEADS, 0].set(ml_b_i[l]).at[SUBLANES:SUBLANES + HEADS, 0].set(ml_b_f[l])
    return {
        "g_mix": row(norm_mix_g[l]), "w_in": _LayerOf(big["w_in"], l), "w_g": _mx(w_g), "w_gt": _mx(w_gt),
        "gb_row": gb_row, "gb_col": gb_col, "gm_v_g": row(gm_v_norm_g[l]), "gm_ws": gm_ws[l], "gm_bst": gm_bs[l].T,
        "gm_ws0": row(jnp.repeat(gm_ws[l][:, 0, 0], HEAD_DIM)), "gm_bs0": row(jnp.repeat(gm_bs[l][:, 0], HEAD_DIM)),
        "conv_w": ml_conv_w[l], "conv_b": row(ml_conv_b[l]),
        "wq": _mx(ml_wq[l]), "wk": _mx(ml_wk[l]), "wv": _mx(ml_wv[l]),
        "wqk": _mx(jnp.concatenate([ml_wq[l], ml_wk[l]], axis=-1)),
        "wqt": _mx(jnp.swapaxes(ml_wq[l], 1, 2)), "wkt": _mx(jnp.swapaxes(ml_wk[l], 1, 2)),
        "out_g": row(ml_out_norm_g[l]), "skip": row(ml_skip[l]), "w_out": _LayerOf(big["w_out"], l),
        "g_xa": row(norm_xa_g[l]), "w_cq": _LayerOf(big["w_cq"], l), "w_co": _LayerOf(big["w_co"], l),
        "g_ffn": row(norm_ffn_g[l]), "w_up": _LayerOf(big["w_up"], l), "w_down": _LayerOf(big["w_down"], l),
    }


def kernel(x_prompt, x_sample, mem_prompt, cache_mem_k, cache_mem_v, state_C, state_n, state_m, state_conv, norm_mix_g, w_in, gm_v_norm_g, gm_ws, gm_bs, ml_conv_w, ml_conv_b, ml_wq, ml_wk, ml_wv, ml_b_i, ml_b_f, ml_out_norm_g, ml_skip, w_out, norm_mem_g, w_ck, w_cv, norm_xa_g, w_cq, w_co, norm_ffn_g, w_up, w_down, norm_f_g):
    depth = w_in.shape[0]
    B, S, Dm = x_prompt.shape
    N = x_sample.shape[0]
    assert x_sample.shape[1] == 1
    Mt = mem_prompt.shape[1]
    W = HEADS * HEAD_DIM
    xa_dh = Dm // HEADS
    gf = norm_f_g.reshape(1, Dm).astype(F32)

    xp = x_prompt
    xs = x_sample.reshape(N, Dm)
    mem2d = mem_prompt.reshape(B * Mt, Dm)
    mk_all, mv_all, mk_split, mv_split = _memkv(mem2d, norm_mem_g.reshape(depth, 1, Dm), _mx(w_ck), _mx(w_cv), HEADS)
    mk_rows = mk_all.reshape(depth * B, Mt, Dm)
    mv_rows = mv_all.reshape(depth * B, Mt, Dm)
    big = {"w_in": _mx(w_in), "w_out": _mx(w_out), "w_cq": _mx(w_cq), "w_co": _mx(w_co), "w_up": _mx(w_up),
           "w_down": _mx(w_down)}
    C_p, n_p, m_p, cv_p = [], [], [], []
    n_s, m_s, cv_s, gv_s = [], [], [], []
    C_s = None
    for l in range(depth):
        p = _layer_params(l, big, norm_mix_g, w_in, gm_v_norm_g, gm_ws, gm_bs, ml_conv_w, ml_conv_b, ml_wq, ml_wk, ml_wv,
                          ml_b_i, ml_b_f, ml_out_norm_g, ml_skip, w_out, norm_mem_g, w_ck, w_cv, norm_xa_g, w_cq,
                          w_co, norm_ffn_g, w_up, w_down)
        final = l == depth - 1

        xp, Cp, np_, mp, cvp = _mix(xp, p)
        xp = _xattn(xp, p["g_xa"], p["w_cq"], mk_rows, mv_rows, p["w_co"], l)
        xp = _ffn(xp.reshape(B * S, Dm), p["g_ffn"], p["w_up"], p["w_down"], gf, final).reshape(B, S, Dm)
        C_p.append(Cp)
        n_p.append(np_)
        m_p.append(mp[:, :, 0])
        cv_p.append(cvp)

        m0p = jnp.zeros((N, LANES), F32).at[:, :HEADS].set(state_m[l])
        (ygm, gmv, cnew, qt, kt, vs, a, bc, nn, mt, wi, og, sc) = _s_front(
            xs, state_conv[l].reshape(N, (ML_CONV - 1) * W), state_n[l].reshape(N, W), m0p, p)
        C_s, qc = _s_state(wi[:, :HEADS].reshape(N * HEADS), state_C, qt, kt, vs, l, C_s)
        x1, hq = _s_back(xs, ygm, a, bc, qc, og, sc, p)
        att = _s_attn(hq, cache_mem_k, cache_mem_v, l)
        xs = _s_post(x1, att, p, gf, final)
        n_s.append(nn.reshape(N, HEADS, HEAD_DIM))
        m_s.append(mt[:, :HEADS])
        cv_s.append(cnew.reshape(N, ML_CONV - 1, W))
        gv_s.append(gmv.reshape(N, 1, W))

    st = lambda a: jnp.stack(a, axis=0)
    return (xp, xs.reshape(N, 1, Dm), mk_split.reshape(depth, B, Mt, HEADS, xa_dh), mv_split.reshape(depth, B, Mt, HEADS, xa_dh),
            st(C_p), st(n_p), st(m_p), st(cv_p), C_s, st(n_s), st(m_s), st(cv_s), st(gv_s))
```

```python
import functools

import jax
import jax.numpy as jnp
from jax import lax
from jax.experimental import pallas as pl
from jax.experimental.pallas import tpu as pltpu

F32 = jnp.float32
MXU_DTYPE = jnp.bfloat16

EPS = 1e-6
NEG = -1e30
LANES = 128
SUBLANES = 8
HEADS = 4
HEAD_DIM = 128
GM_CHUNK = 128
ML_CHUNK = 256
ML_CONV = 4
MIX_TILE = 1024
XA_TILE = 1024
FFN_TILE = 1024
FFN_CHUNK = 1024
STATE_BLOCK = 8
VMEM_LIMIT = 56 * 1024 * 1024


def _cparams(*sem):
    return pltpu.CompilerParams(dimension_semantics=sem, vmem_limit_bytes=VMEM_LIMIT)


def _mx(a):
    return a.astype(MXU_DTYPE)


def _dot(a, b):
    return jnp.dot(_mx(a), _mx(b), preferred_element_type=F32)


def _dot_nt(a, b):
    return lax.dot_general(_mx(a), _mx(b), (((1,), (1,)), ((), ())), preferred_element_type=F32)


def _rms(x, g):
    return x * lax.rsqrt(jnp.mean(x * x, axis=-1, keepdims=True) + EPS) * g


def _log_sigmoid(x):
    return jnp.minimum(x, 0.0) - jnp.log1p(jnp.exp(-jnp.abs(x)))


def _split3(x):
    hi = x.astype(MXU_DTYPE)
    r1 = x - hi.astype(F32)
    mid = r1.astype(MXU_DTYPE)
    lo = (r1 - mid.astype(F32)).astype(MXU_DTYPE)
    return hi, mid, lo


def _full_spec(shape):
    n = len(shape)
    return pl.BlockSpec(shape, lambda *_: (0,) * n)


class _LayerOf:
    def __init__(self, stacked, layer):
        self.stacked, self.layer, self.shape = stacked, layer, tuple(stacked.shape[1:])


def _spec(a, single_buffer=False):
    if isinstance(a, _LayerOf):
        n, layer = len(a.shape), a.layer
        mode = dict(pipeline_mode=pl.Buffered(1)) if single_buffer else {}
        return pl.BlockSpec((None,) + a.shape, lambda *_: (layer,) + (0,) * n, **mode)
    return _full_spec(a.shape)


def _arr(a):
    return a.stacked if isinstance(a, _LayerOf) else a


def _memkv_kernel(m_ref, g_ref, wk_ref, wv_ref, k_ref, v_ref, k4_ref, v4_ref):
    hn = _mx(_rms(m_ref[...], g_ref[0]))
    heads, dh = k4_ref.shape[2:]
    for w_ref, o_ref, o4_ref in ((wk_ref, k_ref, k4_ref), (wv_ref, v_ref, v4_ref)):
        y = jnp.dot(hn, w_ref[0], preferred_element_type=F32)
        o_ref[0] = y
        o4_ref[0] = y.reshape(y.shape[0], heads, dh)


def _memkv(mem2d, g, wk, wv, heads):
    M, D = mem2d.shape
    depth = wk.shape[0]
    tm = min(M, 512)
    wspec = pl.BlockSpec((1, D, D), lambda l, i: (l, 0, 0))
    ospec = pl.BlockSpec((1, tm, D), lambda l, i: (l, i, 0))
    o4spec = pl.BlockSpec((1, tm, heads, D // heads), lambda l, i: (l, i, 0, 0))
    rows = jax.ShapeDtypeStruct((depth, M, D), F32)
    split = jax.ShapeDtypeStruct((depth, M, heads, D // heads), F32)
    return pl.pallas_call(
        _memkv_kernel,
        grid=(depth, M // tm),
        in_specs=[pl.BlockSpec((tm, D), lambda l, i: (i, 0)), pl.BlockSpec((1, 1, D), lambda l, i: (l, 0, 0)), wspec, wspec],
        out_specs=[ospec, ospec, o4spec, o4spec],
        out_shape=[rows, rows, split, split],
        compiler_params=_cparams("parallel", "parallel"),
        name="memkv",
    )(mem2d, g, wk, wv)


def _transpose_exact(eye, rows):
    return sum(lax.dot_general(eye, p, (((1,), (1,)), ((), ())), preferred_element_type=F32) for p in _split3(rows))


def _mix_kernel(x_ref, gmix_ref, win_ref, wgt_ref, gbcol_ref, gmvg_ref, ws_ref, bst_ref,
                cw_ref, cb_ref, wqk_ref, wv_ref, outg_ref, skip_ref, wout_ref,
                xo_ref, c_ref, n_ref, m_ref, conv_ref, xe_s, ygm_s, yml_s, ca_s, m_s, *, tile, width):
    W = width
    L = min(tile, ML_CHUNK)
    D = HEAD_DIM
    R = SUBLANES

    @pl.when(pl.program_id(1) == 0)
    def _init():
        xe_s[0:SUBLANES, :] = jnp.zeros((SUBLANES, W), F32)
        ca_s[...] = jnp.zeros_like(ca_s)
        m_s[...] = jnp.zeros_like(m_s)

    x = x_ref[0]
    hn = _mx(_rms(x, gmix_ref[...]))
    g_row = _dot_nt(wgt_ref[...], hn) + gbcol_ref[...]
    ig_row = g_row[:R]
    lf_row = _log_sigmoid(g_row[R:])

    row_i = lax.broadcasted_iota(jnp.int32, (L, L), 0)
    col_i = lax.broadcasted_iota(jnp.int32, (L, L), 1)
    causal = col_i <= row_i
    triu = _mx(jnp.where(row_i <= col_i, 1.0, 0.0))
    eye = _mx(jnp.where(row_i == col_i, 1.0, 0.0))
    n_ch = tile // L
    chunks = [slice(c * L, (c + 1) * L) for c in range(n_ch)]
    b_all = jnp.concatenate([sum(jnp.dot(p, triu, preferred_element_type=F32) for p in _split3(lf_row[:, rs]))
                             for rs in chunks], axis=1)
    g_all = ig_row - b_all
    pos = lax.broadcasted_iota(jnp.int32, (R, tile), 1) % L

    NB = 2 * D
    starts = [c for lo, hi in ((2 * W, 3 * W), (0, 2 * W), (3 * W, 4 * W)) for c in range(lo, hi, NB)]
    proj_blk = []
    run_max = g_all
    step = 1
    for c0 in starts:
        proj_blk.append(jnp.dot(hn, win_ref[:, c0:c0 + NB], preferred_element_type=F32))
        if step < L:
            run_max = jnp.maximum(run_max, jnp.where(pos >= step, pltpu.roll(run_max, step, 1), NEG))
            step *= 2
    while step < L:
        run_max = jnp.maximum(run_max, jnp.where(pos >= step, pltpu.roll(run_max, step, 1), NEG))
        step *= 2
    proj = jnp.concatenate(proj_blk, axis=1)

    g_r, cols, decay_all = [], [], []
    m_prev = m_s[:, 0:1]
    for rs in chunks:
        b_r = b_all[:, rs]
        g = g_all[:, rs]
        M_r = jnp.maximum(run_max[:, rs], m_prev)
        M_last = M_r[:, L - 1:L]
        g_r.append(g)
        cols.append(_transpose_exact(eye, jnp.concatenate(
            [M_r, jnp.exp(m_prev - M_r), jnp.exp(-(b_r + M_r)), jnp.exp(g - M_last)], axis=0)))
        decay_all.append(jnp.exp(m_prev - M_last))
        m_prev = b_r[:, L - 1:L] + M_last
    m_s[...] = jnp.broadcast_to(m_prev, (R, LANES))

    xm = proj[:, :W]
    xe_s[SUBLANES:SUBLANES + tile, :] = xm
    conv = cb_ref[...] + cw_ref[ML_CONV - 1:ML_CONV, :] * xm
    for j in range(1, ML_CONV):
        conv = conv + cw_ref[ML_CONV - 1 - j:ML_CONV - j, :] * xe_s[SUBLANES - j:SUBLANES - j + tile, :]
    ca = conv * jax.nn.sigmoid(conv)
    conv_ref[0] = xe_s[SUBLANES + tile - (ML_CONV - 1):SUBLANES + tile, :]
    xe_s[0:SUBLANES, :] = xe_s[tile:tile + SUBLANES, :]
    ones_blk = jnp.ones((tile, D), F32)
    k_scale = D ** -0.5
    q, k, va = [], [], []
    for h in range(HEADS):
        cs = slice(h * D, (h + 1) * D)
        qk = jnp.dot(_mx(ca[:, cs]), wqk_ref[h], preferred_element_type=F32)
        q.append(_mx(qk[:, :D]))
        k.append(qk[:, D:] * k_scale)
        v = jnp.dot(_mx(xm[:, cs]), wv_ref[h], preferred_element_type=F32)
        va.append(_mx(jnp.concatenate([v, ones_blk], axis=1)))

    z = jax.nn.gelu(proj[:, W:3 * W])
    u = z[:, :W]
    v_gm = _rms(z[:, W:], gmvg_ref[...])
    G = GM_CHUNK
    g_causal = lax.broadcasted_iota(jnp.int32, (G, G), 1) <= lax.broadcasted_iota(jnp.int32, (G, G), 0)
    n_gm = tile // G
    for h in range(HEADS):
        cs = slice(h * D, (h + 1) * D)
        w_h = _mx(jnp.where(g_causal, ws_ref[h], 0.0))
        b_h = bst_ref[:, h:h + 1]
        for c0 in range(0, n_gm, 2):
            cc = list(range(c0, min(c0 + 2, n_gm)))
            vv = jnp.concatenate([_mx(v_gm[c * G:(c + 1) * G, cs]) for c in cc], axis=1)
            mixed = jnp.dot(w_h, vv, preferred_element_type=F32) + b_h
            for i, c in enumerate(cc):
                rs = slice(c * G, (c + 1) * G)
                ygm_s[rs, cs] = u[rs, cs] * mixed[:, i * D:(i + 1) * D]
    acc = x + jnp.dot(_mx(ygm_s[...]), wout_ref[0:W, :], preferred_element_type=F32)

    pairs = [(c, h) for c in range(n_ch) for h in range(HEADS)]

    def scores(c, h):
        return _dot_nt(q[h][chunks[c]], k[h][chunks[c]])

    Ca = [ca_s[h] for h in range(HEADS)]
    s_next = scores(*pairs[0])
    for i, (c, h) in enumerate(pairs):
        rs = chunks[c]
        cs = slice(h * D, (h + 1) * D)
        s_raw = s_next
        if i + 1 < len(pairs):
            s_next = scores(*pairs[i + 1])
        decay_mat = jnp.exp(jnp.where(causal, g_r[c][h:h + 1, :] - cols[c][:, h:h + 1], NEG))
        sv = jnp.dot(_mx(s_raw * decay_mat), va[h][rs], preferred_element_type=F32)
        qc = jnp.dot(q[h][rs], _mx(Ca[h]), preferred_element_type=F32)
        w_inter = cols[c][:, R + h:R + h + 1]
        floor = cols[c][:, 2 * R + h:2 * R + h + 1]
        w_s = cols[c][:, 3 * R + h:3 * R + h + 1]
        Ca[h] = decay_all[c][h:h + 1, :] * Ca[h] + jnp.dot(_mx(jnp.transpose(k[h][rs] * w_s)), va[h][rs],
                                                           preferred_element_type=F32)
        num = sv[:, :D] + w_inter * qc[:, :D]
        den = sv[:, D:] + w_inter * qc[:, D:]
        hc = num / jnp.maximum(jnp.abs(den), floor)
        hcn = _rms(hc, outg_ref[:, cs])
        yml_s[rs, cs] = jax.nn.sigmoid(proj[rs, 3 * W + h * D:3 * W + (h + 1) * D]) * (hcn + skip_ref[:, cs] * ca[rs, cs])
    for h in range(HEADS):
        ca_s[h] = Ca[h]

    xo_ref[0] = acc + jnp.dot(_mx(yml_s[...]), wout_ref[W:2 * W, :], preferred_element_type=F32)

    @pl.when(pl.program_id(1) == pl.num_programs(1) - 1)
    def _emit_state():
        for h in range(HEADS):
            c_ref[0, h] = ca_s[h, :, :D]
            n_ref[0, h:h + 1, :] = jnp.transpose(ca_s[h, :, D:])[0:1, :]
        m_ref[0] = m_s[0:HEADS, :]


def _mix(x, p):
    B, S, Dm = x.shape
    W = HEADS * HEAD_DIM
    tile = min(S, MIX_TILE)
    assert S % tile == 0 and tile % GM_CHUNK == 0 and tile % min(tile, ML_CHUNK) == 0
    weights = (p["g_mix"], p["w_in"], p["w_gt"], p["gb_col"], p["gm_v_g"], p["gm_ws"],
               p["gm_bst"], p["conv_w"], p["conv_b"], p["wqk"], p["wv"], p["out_g"], p["skip"], p["w_out"])
    return pl.pallas_call(
        functools.partial(_mix_kernel, tile=tile, width=W),
        grid=(B, S // tile),
        in_specs=[pl.BlockSpec((1, tile, Dm), lambda b, s: (b, s, 0))] + [_spec(w) for w in weights],
        out_specs=[pl.BlockSpec((1, tile, Dm), lambda b, s: (b, s, 0)),
                   pl.BlockSpec((1, HEADS, HEAD_DIM, HEAD_DIM), lambda b, s: (b, 0, 0, 0)),
                   pl.BlockSpec((1, HEADS, HEAD_DIM), lambda b, s: (b, 0, 0)),
                   pl.BlockSpec((1, HEADS, LANES), lambda b, s: (b, 0, 0)),
                   pl.BlockSpec((1, ML_CONV - 1, W), lambda b, s: (b, 0, 0))],
        out_shape=[jax.ShapeDtypeStruct((B, S, Dm), F32),
                   jax.ShapeDtypeStruct((B, HEADS, HEAD_DIM, HEAD_DIM), F32),
                   jax.ShapeDtypeStruct((B, HEADS, HEAD_DIM), F32),
                   jax.ShapeDtypeStruct((B, HEADS, LANES), F32),
                   jax.ShapeDtypeStruct((B, ML_CONV - 1, W), F32)],
        scratch_shapes=[pltpu.VMEM((tile + SUBLANES, W), F32), pltpu.VMEM((tile, W), F32), pltpu.VMEM((tile, W), F32),
                        pltpu.VMEM((HEADS, HEAD_DIM, 2 * HEAD_DIM), F32), pltpu.VMEM((SUBLANES, LANES), F32)],
        compiler_params=_cparams("parallel", "arbitrary"),
        name="mix",
    )(x, *map(_arr, weights))


def _xattn_kernel(x_ref, g_ref, wq_ref, k_ref, v_ref, wo_ref, o_ref, *, heads):
    x = x_ref[0]
    hq = jnp.dot(_mx(_rms(x, g_ref[...])), wq_ref[...], preferred_element_type=F32)
    dh = x.shape[1] // heads
    scale = dh ** -0.5
    cols = [slice(h * dh, (h + 1) * dh) for h in range(heads)]

    def scores(h):
        return _dot_nt(hq[:, cols[h]], k_ref[0, :, cols[h]]) * scale

    acc = x
    pairs = [list(range(h0, min(h0 + 2, heads))) for h0 in range(0, heads, 2)]
    sc_next = [scores(h) for h in pairs[0]]
    for i, hs in enumerate(pairs):
        sc = sc_next
        if i + 1 < len(pairs):
            sc_next = [scores(h) for h in pairs[i + 1]]
        es = [jnp.exp(s_ - jnp.max(s_, axis=-1, keepdims=True)) for s_ in sc]
        probs = [e / jnp.sum(e, axis=-1, keepdims=True) for e in es]
        att = jnp.concatenate([_dot(p_, v_ref[0, :, cols[h]]) for p_, h in zip(probs, hs)], axis=-1)
        acc = acc + jnp.dot(_mx(att), wo_ref[hs[0] * dh:(hs[-1] + 1) * dh, :], preferred_element_type=F32)
    o_ref[0] = acc


def _xattn(x, g, wq, mk, mv, wo, layer):
    B, S, Dm = x.shape
    Mt = mk.shape[1]
    kv = pl.BlockSpec((1, Mt, Dm), lambda b, s: (layer * B + b, 0, 0))
    tile = min(S, XA_TILE)
    assert S % tile == 0
    return pl.pallas_call(
        functools.partial(_xattn_kernel, heads=HEADS),
        grid=(B, S // tile),
        in_specs=[pl.BlockSpec((1, tile, Dm), lambda b, s: (b, s, 0)), _full_spec(g.shape), _spec(wq), kv, kv, _spec(wo)],
        out_specs=pl.BlockSpec((1, tile, Dm), lambda b, s: (b, s, 0)),
        out_shape=jax.ShapeDtypeStruct((B, S, Dm), F32),
        compiler_params=_cparams("parallel", "parallel"),
        name="xattn",
    )(x, g, _arr(wq), mk, mv, _arr(wo))


def _ffn_kernel(x_ref, g_ref, wu_ref, wd_ref, gf_ref, o_ref, *, final):
    x = x_ref[...]
    hn = _mx(_rms(x, g_ref[...]))
    acc = x
    dff = wu_ref.shape[1]
    fc = min(dff, FFN_CHUNK)
    for c in range(dff // fc):
        up = jnp.dot(hn, wu_ref[:, c * fc:(c + 1) * fc], preferred_element_type=F32)
        act = jnp.square(jnp.maximum(up, 0.0))
        acc = acc + jnp.dot(_mx(act), wd_ref[c * fc:(c + 1) * fc, :], preferred_element_type=F32)
    o_ref[...] = _rms(acc, gf_ref[...]) if final else acc


def _ffn(x2d, g, wu, wd, gf, final):
    M, Dm = x2d.shape
    tile = min(M, FFN_TILE)
    assert M % tile == 0
    return pl.pallas_call(
        functools.partial(_ffn_kernel, final=final),
        grid=(M // tile,),
        in_specs=[pl.BlockSpec((tile, Dm), lambda i: (i, 0)), _full_spec(g.shape), _spec(wu, True), _spec(wd, True), _full_spec(gf.shape)],
        out_specs=pl.BlockSpec((tile, Dm), lambda i: (i, 0)),
        out_shape=jax.ShapeDtypeStruct((M, Dm), F32),
        compiler_params=_cparams("parallel"),
        name="ffn",
    )(x2d, g, _arr(wu), _arr(wd), gf)


def _s_front_kernel(x_ref, gmix_ref, win_ref, wg_ref, gbrow_ref, gmvg_ref, ws0_ref, bs0_ref, cw_ref, cb_ref,
                    cst_ref, wq_ref, wk_ref, wv_ref, wqt_ref, wkt_ref, n0_ref, m0_ref, skip_ref,
                    ygm_ref, gmv_ref, cnew_ref, qt_ref, kt_ref, vs_ref, a_ref, bc_ref, nn_ref, mt_ref, wi_ref,
                    og_ref, sc_ref, *, width):
    W = width
    D = HEAD_DIM
    x = x_ref[...]
    hn = _mx(_rms(x, gmix_ref[...]))
    proj = jnp.dot(hn, win_ref[:, :4 * W], preferred_element_type=F32)
    g_col = jnp.dot(hn, wg_ref[...], preferred_element_type=F32) + gbrow_ref[...]
    ig = g_col[:, :LANES]
    lf = _log_sigmoid(g_col[:, LANES:])

    z = jax.nn.gelu(proj[:, :2 * W])
    v_gm = _rms(z[:, W:], gmvg_ref[...])
    gmv_ref[...] = v_gm
    ygm_ref[...] = z[:, :W] * (ws0_ref[...] * v_gm + bs0_ref[...])

    xm = proj[:, 2 * W:3 * W]
    conv = cb_ref[...] + cw_ref[ML_CONV - 1:ML_CONV, :] * xm
    for j in range(ML_CONV - 1):
        conv = conv + cw_ref[j:j + 1, :] * cst_ref[:, j * W:(j + 1) * W]
    ca = conv * jax.nn.sigmoid(conv)
    cnew_ref[:, :(ML_CONV - 2) * W] = cst_ref[:, W:]
    cnew_ref[:, (ML_CONV - 2) * W:] = xm
    og_ref[...] = jax.nn.sigmoid(proj[:, 3 * W:])
    sc_ref[...] = skip_ref[...] * ca

    a = lf + m0_ref[...]
    mt = jnp.maximum(a, ig)
    w_inter = jnp.exp(a - mt)
    w_in_gate = jnp.exp(ig - mt)
    floor = jnp.exp(-mt)
    mt_ref[...] = mt
    wi_ref[...] = w_inter
    k_scale = D ** -0.5
    for h in range(HEADS):
        cs = slice(h * D, (h + 1) * D)
        ca_h = _mx(ca[:, cs])
        q = jnp.dot(ca_h, wq_ref[h], preferred_element_type=F32)
        k = jnp.dot(ca_h, wk_ref[h], preferred_element_type=F32) * k_scale
        v = jnp.dot(_mx(xm[:, cs]), wv_ref[h], preferred_element_type=F32)
        qt_ref[h] = _dot_nt(wqt_ref[h], ca_h)
        kt_ref[h] = _dot_nt(wkt_ref[h], ca_h) * k_scale
        wi_h = w_inter[:, h:h + 1]
        wg_h = w_in_gate[:, h:h + 1]
        n0 = n0_ref[:, cs]
        s = jnp.sum(q * k, axis=1, keepdims=True) * wg_h
        den = s + wi_h * jnp.sum(q * n0, axis=1, keepdims=True)
        dnm = jnp.maximum(jnp.abs(den), floor[:, h:h + 1])
        a_ref[:, cs] = s * v / dnm
        bc_ref[:, cs] = jnp.broadcast_to(wi_h / dnm, (x.shape[0], D))
        vs_ref[:, cs] = wg_h * v
        nn_ref[:, cs] = wi_h * n0 + wg_h * k


def _s_front(x, cst, n0, m0p, p):
    N, Dm = x.shape
    W = HEADS * HEAD_DIM
    ins = (x, p["g_mix"], p["w_in"], p["w_g"], p["gb_row"], p["gm_v_g"], p["gm_ws0"], p["gm_bs0"],
           p["conv_w"], p["conv_b"], cst, p["wq"], p["wk"], p["wv"], p["wqt"], p["wkt"], n0, m0p, p["skip"])
    row = jax.ShapeDtypeStruct((N, W), F32)
    colw = jax.ShapeDtypeStruct((N, LANES), F32)
    tr = jax.ShapeDtypeStruct((HEADS, HEAD_DIM, N), F32)
    outs = [row, row, jax.ShapeDtypeStruct((N, (ML_CONV - 1) * W), F32), tr, tr, row, row, row, row, colw, colw, row, row]
    return pl.pallas_call(
        functools.partial(_s_front_kernel, width=W),
        grid=(1,),
        in_specs=[_spec(a) for a in ins],
        out_specs=[_full_spec(o.shape) for o in outs],
        out_shape=outs,
        compiler_params=_cparams("arbitrary"),
        name="s_front",
    )(*map(_arr, ins))


def _s_state_kernel(dec_ref, c_ref, qt_ref, kt_ref, vs_ref, *rest, block, nseq):
    cn_ref, qc_ref = rest[-2:]
    D = HEAD_DIM

    @pl.when(pl.program_id(0) == 0)
    def _update():
        base = pl.program_id(1) * block
        shift = (nseq - base) % nseq
        for h in range(HEADS):
            qt = pltpu.roll(qt_ref[h], shift, 1)
            kt = pltpu.roll(kt_ref[h], shift, 1)
            for j in range(block):
                C = c_ref[0, j, h]
                qc_ref[j:j + 1, h * D:(h + 1) * D] = jnp.sum(C * qt[:, j:j + 1], axis=0, keepdims=True)
                decay = dec_ref[(base + j) * HEADS + h]
                cn_ref[0, j, h] = decay * C + kt[:, j:j + 1] * vs_ref[j:j + 1, h * D:(h + 1) * D]

    @pl.when(pl.program_id(0) > 0)
    def _fill():
        cn_ref[...] = jnp.zeros_like(cn_ref)


def _s_state(dec, C_all, qt, kt, vs, layer, C_new_all):
    depth, N = C_all.shape[:2]
    W = HEADS * HEAD_DIM
    bb = min(N, STATE_BLOCK)
    assert N % bb == 0
    nblk = N // bb
    passes = depth if C_new_all is None else 1
    blk = lambda p, i: jnp.where(p == 0, i, nblk - 1)
    cin = pl.BlockSpec((1, bb, HEADS, HEAD_DIM, HEAD_DIM), lambda p, i: (layer, blk(p, i), 0, 0, 0))
    cout = pl.BlockSpec((1, bb, HEADS, HEAD_DIM, HEAD_DIM), lambda p, i: ((layer + p) % depth, i, 0, 0, 0))
    rblk = pl.BlockSpec((bb, W), lambda p, i: (blk(p, i), 0))
    ins = [dec, C_all, qt, kt, vs]
    in_specs = [pl.BlockSpec(memory_space=pltpu.SMEM), cin, _full_spec(qt.shape), _full_spec(kt.shape), rblk]
    aliases = {}
    if C_new_all is not None:
        ins.append(C_new_all)
        in_specs.append(pl.BlockSpec(memory_space=pl.ANY))
        aliases = {len(ins) - 1: 0}
    return pl.pallas_call(
        functools.partial(_s_state_kernel, block=bb, nseq=N),
        grid=(passes, nblk),
        in_specs=in_specs,
        out_specs=[cout, rblk],
        out_shape=[jax.ShapeDtypeStruct(C_all.shape, F32), jax.ShapeDtypeStruct((N, W), F32)],
        input_output_aliases=aliases,
        compiler_params=_cparams("arbitrary", "arbitrary"),
        name="s_state",
    )(*ins)


def _s_back_kernel(x_ref, ygm_ref, a_ref, bc_ref, qc_ref, og_ref, sc_ref, outg_ref, wout_ref, gxa_ref, wcq_ref,
                   x1_ref, hq_ref, *, width):
    W = width
    D = HEAD_DIM
    hc = a_ref[...] + bc_ref[...] * qc_ref[...]
    parts = [ygm_ref[...]]
    for h in range(HEADS):
        cs = slice(h * D, (h + 1) * D)
        parts.append(og_ref[:, cs] * (_rms(hc[:, cs], outg_ref[:, cs]) + sc_ref[:, cs]))
    y = jnp.concatenate(parts, axis=-1)
    x1 = x_ref[...] + jnp.dot(_mx(y), wout_ref[...], preferred_element_type=F32)
    x1_ref[...] = x1
    hq_ref[...] = jnp.dot(_mx(_rms(x1, gxa_ref[...])), wcq_ref[...], preferred_element_type=F32)


def _s_back(x, ygm, a, bc, qc, og, sc, p):
    N, Dm = x.shape
    ins = (x, ygm, a, bc, qc, og, sc, p["out_g"], p["w_out"], p["g_xa"], p["w_cq"])
    out = jax.ShapeDtypeStruct((N, Dm), F32)
    return pl.pallas_call(
        functools.partial(_s_back_kernel, width=HEADS * HEAD_DIM),
        grid=(1,),
        in_specs=[_spec(a_) for a_ in ins],
        out_specs=[_full_spec(out.shape)] * 2,
        out_shape=[out, out],
        compiler_params=_cparams("arbitrary"),
        name="s_back",
    )(*map(_arr, ins))


def _s_attn_kernel(q_ref, k_ref, v_ref, o_ref, *, block, heads, scale):
    for j in range(block):
        prod = k_ref[0, j] * q_ref[0, j][None]
        prod = prod + pltpu.roll(prod, heads, 1)
        sc = jnp.sum(prod, axis=-1, keepdims=True) * scale
        e = jnp.exp(sc - jnp.max(sc, axis=0, keepdims=True))
        o_ref[0, j] = jnp.sum(v_ref[0, j] * e, axis=0) / jnp.sum(e, axis=0)


def _s_attn(hq, K_all, V_all, layer):
    N, Dm = hq.shape
    depth, _, Mt, H, dh = K_all.shape
    assert dh == 2 * LANES
    bb = min(N, STATE_BLOCK)
    assert N % bb == 0
    halves = lambda a, lead: a.reshape(lead + (H, 2, LANES)).swapaxes(-2, -3).reshape(lead + (2 * H, LANES))
    kv = pl.BlockSpec((1, bb, Mt, 2 * H, LANES), lambda i: (layer, i, 0, 0, 0))
    qo = pl.BlockSpec((1, bb, 2 * H, LANES), lambda i: (i, 0, 0, 0))
    out = pl.pallas_call(
        functools.partial(_s_attn_kernel, block=bb, heads=H, scale=dh ** -0.5),
        grid=(N // bb,),
        in_specs=[qo, kv, kv],
        out_specs=qo,
        out_shape=jax.ShapeDtypeStruct((N // bb, bb, 2 * H, LANES), F32),
        compiler_params=_cparams("parallel"),
        name="s_attn",
    )(halves(hq, (N // bb, bb)), halves(K_all, (depth, N, Mt)), halves(V_all, (depth, N, Mt)))
    return out.reshape(N, 2, H, LANES).swapaxes(1, 2).reshape(N, Dm)


def _s_post_kernel(x1_ref, att_ref, wco_ref, gffn_ref, wu_ref, wd_ref, gf_ref, o_ref, hn_s, *, final):
    i = pl.program_id(0)

    @pl.when(i == 0)
    def _first():
        x2 = x1_ref[...] + jnp.dot(_mx(att_ref[...]), wco_ref[...], preferred_element_type=F32)
        o_ref[...] = x2
        hn_s[...] = _rms(x2, gffn_ref[...])

    up = jnp.dot(_mx(hn_s[...]), wu_ref[...], preferred_element_type=F32)
    act = jnp.square(jnp.maximum(up, 0.0))
    o_ref[...] += jnp.dot(_mx(act), wd_ref[...], preferred_element_type=F32)

    if final:
        @pl.when(i == pl.num_programs(0) - 1)
        def _last():
            o_ref[...] = _rms(o_ref[...], gf_ref[...])


def _s_post(x1, att, p, gf, final):
    N, Dm = x1.shape
    dff = p["w_up"].shape[1]
    fc = min(dff, FFN_CHUNK)
    layer = p["w_up"].layer
    return pl.pallas_call(
        functools.partial(_s_post_kernel, final=final),
        grid=(dff // fc,),
        in_specs=[_full_spec((N, Dm)), _full_spec((N, Dm)), _spec(p["w_co"]), _full_spec((1, Dm)),
                  pl.BlockSpec((None, Dm, fc), lambda i: (layer, 0, i)), pl.BlockSpec((None, fc, Dm), lambda i: (layer, i, 0)),
                  _full_spec((1, Dm))],
        out_specs=_full_spec((N, Dm)),
        out_shape=jax.ShapeDtypeStruct((N, Dm), F32),
        scratch_shapes=[pltpu.VMEM((N, Dm), F32)],
        compiler_params=_cparams("arbitrary"),
        name="s_post",
    )(x1, att, _arr(p["w_co"]), p["g_ffn"], _arr(p["w_up"]), _arr(p["w_down"]), gf)


def _layer_params(l, big, norm_mix_g, w_in, gm_v_norm_g, gm_ws, gm_bs, ml_conv_w, ml_conv_b, ml_wq, ml_wk, ml_wv,
                  ml_b_i, ml_b_f, ml_out_norm_g, ml_skip, w_out, norm_mem_g, w_ck, w_cv, norm_xa_g, w_cq, w_co,
                  norm_ffn_g, w_up, w_down):
    W = HEADS * HEAD_DIM
    Dm = w_in.shape[1]
    row = lambda a: a.reshape(1, -1).astype(F32)
    wi = w_in[l]
    w_gate = wi[:, 4 * W:]
    w_g = jnp.zeros((Dm, 2 * LANES), F32).at[:, :HEADS].set(w_gate[:, :HEADS]).at[:, LANES:LANES + HEADS].set(w_gate[:, HEADS:])
    w_gt = jnp.zeros((2 * SUBLANES, Dm), F32).at[:HEADS].set(w_gate[:, :HEADS].T).at[SUBLANES:SUBLANES + HEADS].set(w_gate[:, HEADS:].T)
    gb_row = jnp.zeros((1, 2 * LANES), F32).at[0, :HEADS].set(ml_b_i[l]).at[0, LANES:LANES + HEADS].set(ml_b_f[l])
    gb_col = jnp.zeros((2 * SUBLANES, 1), F32).at[:HEADS, 0].set(ml_b_i[l]).at[SUBLANES:SUBLANES + HEADS, 0].set(ml_b_f[l])
    return {
        "g_mix": row(norm_mix_g[l]), "w_in": _LayerOf(big["w_in"], l), "w_g": _mx(w_g), "w_gt": _mx(w_gt),
        "gb_row": gb_row, "gb_col": gb_col, "gm_v_g": row(gm_v_norm_g[l]), "gm_ws": gm_ws[l], "gm_bst": gm_bs[l].T,
        "gm_ws0": row(jnp.repeat(gm_ws[l][:, 0, 0], HEAD_DIM)), "gm_bs0": row(jnp.repeat(gm_bs[l][:, 0], HEAD_DIM)),
        "conv_w": ml_conv_w[l], "conv_b": row(ml_conv_b[l]),
        "wq": _mx(ml_wq[l]), "wk": _mx(ml_wk[l]), "wv": _mx(ml_wv[l]),
        "wqk": _mx(jnp.concatenate([ml_wq[l], ml_wk[l]], axis=-1)),
        "wqt": _mx(jnp.swapaxes(ml_wq[l], 1, 2)), "wkt": _mx(jnp.swapaxes(ml_wk[l], 1, 2)),
        "out_g": row(ml_out_norm_g[l]), "skip": row(ml_skip[l]), "w_out": _LayerOf(big["w_out"], l),
        "g_xa": row(norm_xa_g[l]), "w_cq": _LayerOf(big["w_cq"], l), "w_co": _LayerOf(big["w_co"], l),
        "g_ffn": row(norm_ffn_g[l]), "w_up": _LayerOf(big["w_up"], l), "w_down": _LayerOf(big["w_down"], l),
    }


def kernel(x_prompt, x_sample, mem_prompt, cache_mem_k, cache_mem_v, state_C, state_n, state_m, state_conv, norm_mix_g, w_in, gm_v_norm_g, gm_ws, gm_bs, ml_conv_w, ml_conv_b, ml_wq, ml_wk, ml_wv, ml_b_i, ml_b_f, ml_out_norm_g, ml_skip, w_out, norm_mem_g, w_ck, w_cv, norm_xa_g, w_cq, w_co, norm_ffn_g, w_up, w_down, norm_f_g):
    depth = w_in.shape[0]
    B, S, Dm = x_prompt.shape
    N = x_sample.shape[0]
    assert x_sample.shape[1] == 1
    Mt = mem_prompt.shape[1]
    W = HEADS * HEAD_DIM
    xa_dh = Dm // HEADS
    gf = norm_f_g.reshape(1, Dm).astype(F32)

    xp = x_prompt
    xs = x_sample.reshape(N, Dm)
    mem2d = mem_prompt.reshape(B * Mt, Dm)
    mk_all, mv_all, mk_split, mv_split = _memkv(mem2d, norm_mem_g.reshape(depth, 1, Dm), _mx(w_ck), _mx(w_cv), HEADS)
    mk_rows = mk_all.reshape(depth * B, Mt, Dm)
    mv_rows = mv_all.reshape(depth * B, Mt, Dm)
    big = {"w_in": _mx(w_in), "w_out": _mx(w_out), "w_cq": _mx(w_cq), "w_co": _mx(w_co), "w_up": _mx(w_up),
           "w_down": _mx(w_down)}
    C_p, n_p, m_p, cv_p = [], [], [], []
    n_s, m_s, cv_s, gv_s = [], [], [], []
    C_s = None
    for l in range(depth):
        p = _layer_params(l, big, norm_mix_g, w_in, gm_v_norm_g, gm_ws, gm_bs, ml_conv_w, ml_conv_b, ml_wq, ml_wk, ml_wv,
                          ml_b_i, ml_b_f, ml_out_norm_g, ml_skip, w_out, norm_mem_g, w_ck, w_cv, norm_xa_g, w_cq,
                          w_co, norm_ffn_g, w_up, w_down)
        final = l == depth - 1

        xp, Cp, np_, mp, cvp = _mix(xp, p)
        xp = _xattn(xp, p["g_xa"], p["w_cq"], mk_rows, mv_rows, p["w_co"], l)
        xp = _ffn(xp.reshape(B * S, Dm), p["g_ffn"], p["w_up"], p["w_down"], gf, final).reshape(B, S, Dm)
        C_p.append(Cp)
        n_p.append(np_)
        m_p.append(mp[:, :, 0])
        cv_p.append(cvp)

        m0p = jnp.zeros((N, LANES), F32).at[:, :HEADS].set(state_m[l])
        (ygm, gmv, cnew, qt, kt, vs, a, bc, nn, mt, wi, og, sc) = _s_front(
            xs, state_conv[l].reshape(N, (ML_CONV - 1) * W), state_n[l].reshape(N, W), m0p, p)
        C_s, qc = _s_state(wi[:, :HEADS].reshape(N * HEADS), state_C, qt, kt, vs, l, C_s)
        x1, hq = _s_back(xs, ygm, a, bc, qc, og, sc, p)
        att = _s_attn(hq, cache_mem_k, cache_mem_v, l)
        xs = _s_post(x1, att, p, gf, final)
        n_s.append(nn.reshape(N, HEADS, HEAD_DIM))
        m_s.append(mt[:, :HEADS])
        cv_s.append(cnew.reshape(N, ML_CONV - 1, W))
        gv_s.append(gmv.reshape(N, 1, W))

    st = lambda a: jnp.stack(a, axis=0)
    return (xp, xs.reshape(N, 1, Dm), mk_split.reshape(depth, B, Mt, HEADS, xa_dh), mv_split.reshape(depth, B, Mt, HEADS, xa_dh),
            st(C_p), st(n_p), st(m_p), st(cv_p), C_s, st(n_s), st(m_s), st(cv_s), st(gv_s))
```
